```python
import math
import jax, jax.numpy as jnp
from jax import lax
import numpy as np

D_MODEL = 2048
BATCH = 8
SEQ = 4096
DEPTH = 1

GRID_W = 64
NA_HEADS = 16
NA_HEAD_DIM = 64
NA_WIDTH = NA_HEADS * NA_HEAD_DIM
NA_WIN_ROWS = 8
NA_WIN_COLS = 16
NA_QBLOCK_COLS = 16
NA_KBLOCK_COLS = NA_QBLOCK_COLS + NA_WIN_COLS
ML_HEADS = 4
ML_HEAD_DIM = 256
ML_WIDTH = ML_HEADS * ML_HEAD_DIM
ML_CHUNK = 128
CONV_WIDTH = 3
D_FF = 5504
LN_EPS = 1e-5
NEG_BIG = -1e30
DEEPNORM_ALPHA = (2 * DEPTH) ** 0.25
DEEPNORM_BETA = (8 * DEPTH) ** -0.25
IN_SIZES = (3 * NA_WIDTH, ML_WIDTH, ML_WIDTH, ML_WIDTH, ML_WIDTH, 4 * ML_HEADS, 2 * D_MODEL)
IN_WIDTH = sum(IN_SIZES)
IN_SPLIT_POINTS = tuple(int(v) for v in np.cumsum(IN_SIZES)[:-1])

kernel_name = "hybrid_natten_mlstm_convglu_deepnorm"


def layer_norm(x, g, b):
    xf = x.astype(jnp.float32)
    mu = jnp.mean(xf, axis=-1, keepdims=True)
    var = jnp.mean(jnp.square(xf - mu), axis=-1, keepdims=True)
    y = (xf - mu) * lax.rsqrt(var + LN_EPS)
    return (y * g + b).astype(x.dtype)


def dwconv_centred(x, w, b):
    s = x.shape[1]
    pad = CONV_WIDTH // 2
    xp = jnp.pad(x, ((0, 0), (pad, pad), (0, 0)))
    y = xp[:, 0:s] * w[0]
    for i in range(1, CONV_WIDTH):
        y = y + xp[:, i:i + s] * w[i]
    return y + b


def neighborhood_attention(q, k, v, rpb):
    bsz, s, h, d = q.shape
    rows = s // GRID_W
    kh = min(NA_WIN_ROWS, rows)
    nqb = GRID_W // NA_QBLOCK_COLS

    def to_grid(t):
        return t.reshape(bsz, rows, GRID_W, h, d).transpose(0, 3, 1, 2, 4)

    qg = to_grid(q) * (d ** -0.5)
    kg = to_grid(k)
    vg = to_grid(v)

    qcol = np.arange(GRID_W).reshape(nqb, NA_QBLOCK_COLS)
    kstart = np.clip(qcol[:, 0] - NA_WIN_COLS // 2, 0, GRID_W - NA_KBLOCK_COLS)
    kcol = kstart[:, None] + np.arange(NA_KBLOCK_COLS)
    wstart = np.clip(qcol - NA_WIN_COLS // 2, 0, GRID_W - NA_WIN_COLS)
    col_valid = (kcol[:, None, :] >= wstart[..., None]) & (kcol[:, None, :] < wstart[..., None] + NA_WIN_COLS)
    dc_idx = np.clip(kcol[:, None, :] - qcol[:, :, None] + NA_WIN_COLS - 1, 0, 2 * NA_WIN_COLS - 2)
    col_bias = rpb[:, :, dc_idx]
    valid = jnp.asarray(col_valid)[:, :, None, :]

    def row_step(r):
        rs = jnp.clip(r - kh // 2, 0, rows - kh)
        krows = lax.dynamic_slice_in_dim(kg, rs, kh, axis=2)
        vrows = lax.dynamic_slice_in_dim(vg, rs, kh, axis=2)
        kb = krows[:, :, :, kcol, :]
        vb = vrows[:, :, :, kcol, :]
        qr = lax.dynamic_index_in_dim(qg, r, axis=2, keepdims=False).reshape(bsz, h, nqb, NA_QBLOCK_COLS, d)
        sc = jnp.einsum('bhjqd,bhrjkd->bhjqrk', qr, kb).astype(jnp.float32)
        dr_idx = rs + jnp.arange(kh) - r + (NA_WIN_ROWS - 1)
        bias = jnp.take(col_bias, dr_idx, axis=1).transpose(0, 2, 3, 1, 4)
        sc = jnp.where(valid, sc + bias.astype(jnp.float32), NEG_BIG)
        p = jax.nn.softmax(sc.reshape(bsz, h, nqb, NA_QBLOCK_COLS, kh * NA_KBLOCK_COLS), axis=-1)
        p = p.reshape(sc.shape).astype(vb.dtype)
        o = jnp.einsum('bhjqrk,bhrjkd->bhjqd', p, vb)
        return o.reshape(bsz, h, GRID_W, d)

    out = lax.map(row_step, jnp.arange(rows))
    return out.transpose(1, 0, 3, 2, 4).reshape(bsz, s, h * d)


def mlstm_chunkwise(q, k, v, logi, logf):
    g, bsz, h, s, dk = q.shape
    dv = v.shape[-1]
    nc = s // ML_CHUNK

    def chunks(t):
        return jnp.moveaxis(t.reshape(t.shape[:3] + (nc, ML_CHUNK) + t.shape[4:]), 3, 0)

    tri = jnp.tril(jnp.ones((ML_CHUNK, ML_CHUNK), dtype=bool))

    def step(carry, xs):
        c_st, n_st, m_st = carry
        qc, kc, vc, li, lf = xs
        b = jnp.cumsum(lf, axis=-1)
        dlog = jnp.where(tri, b[..., :, None] - b[..., None, :] + li[..., None, :], NEG_BIG)
        inter = b + m_st[..., None]
        m_t = jnp.maximum(jnp.max(dlog, axis=-1), inter)
        dw = jnp.exp(dlog - m_t[..., None])
        w_inter = jnp.exp(inter - m_t)
        sqk = jnp.einsum('gbhtd,gbhsd->gbhts', qc, kc) * dw
        num = w_inter[..., None] * jnp.einsum('gbhtd,gbhde->gbhte', qc, c_st) + jnp.einsum('gbhts,gbhse->gbhte', sqk, vc)
        den = w_inter * jnp.einsum('gbhtd,gbhd->gbht', qc, n_st) + jnp.sum(sqk, axis=-1)
        hc = num / jnp.maximum(jnp.abs(den), jnp.exp(-m_t))[..., None]
        b_last = b[..., -1]
        logw = b_last[..., None] - b + li
        m_new = jnp.maximum(b_last + m_st, jnp.max(logw, axis=-1))
        ws = jnp.exp(logw - m_new[..., None])
        decay = jnp.exp(b_last + m_st - m_new)
        kw = kc * ws[..., None]
        c_new = decay[..., None, None] * c_st + jnp.einsum('gbhsd,gbhse->gbhde', kw, vc)
        n_new = decay[..., None] * n_st + jnp.sum(kw, axis=-2)
        return (c_new, n_new, m_new), hc

    init = (jnp.zeros((g, bsz, h, dk, dv), jnp.float32),
            jnp.zeros((g, bsz, h, dk), jnp.float32),
            jnp.full((g, bsz, h), NEG_BIG, jnp.float32))
    _, hs = lax.scan(step, init, (chunks(q), chunks(k), chunks(v), chunks(logi), chunks(logf)))
    return jnp.moveaxis(hs, 0, 3).reshape(g, bsz, h, s, dv)


def mlstm_branch(q_pre, k_pre, v_in, o_pre, gate_pre, conv_w, conv_b, igate_b, fgate_b, norm_w):
    bsz, s, _ = q_pre.shape
    qk = jax.nn.silu(dwconv_centred(jnp.concatenate([q_pre, k_pre], axis=-1), conv_w, conv_b))
    q, k = jnp.split(qk, 2, axis=-1)

    def heads(t):
        return t.reshape(bsz, s, ML_HEADS, ML_HEAD_DIM).transpose(0, 2, 1, 3).astype(jnp.float32)

    q = heads(q)
    k = heads(k) * (ML_HEAD_DIM ** -0.5)
    v = heads(v_in)
    gp = gate_pre.astype(jnp.float32)
    ig = (gp[..., :2 * ML_HEADS] + igate_b).reshape(bsz, s, 2, ML_HEADS).transpose(2, 0, 3, 1)
    fg = (gp[..., 2 * ML_HEADS:] + fgate_b).reshape(bsz, s, 2, ML_HEADS).transpose(2, 0, 3, 1)
    logf = jax.nn.log_sigmoid(fg)
    flip = lambda t: jnp.flip(t, axis=2)
    q2 = jnp.stack([q, flip(q)])
    k2 = jnp.stack([k, flip(k)])
    v2 = jnp.stack([v, flip(v)])
    li2 = jnp.stack([ig[0], flip(ig[1])])
    lf2 = jnp.stack([logf[0], flip(logf[1])])
    h2 = mlstm_chunkwise(q2, k2, v2, li2, lf2)
    hsum = h2[0] + jnp.flip(h2[1], axis=2)
    mu = jnp.mean(hsum, axis=-1, keepdims=True)
    var = jnp.mean(jnp.square(hsum - mu), axis=-1, keepdims=True)
    hn = (hsum - mu) * lax.rsqrt(var + LN_EPS) * norm_w.reshape(ML_HEADS, 1, ML_HEAD_DIM)
    hn = hn.transpose(0, 2, 1, 3).reshape(bsz, s, ML_WIDTH).astype(q_pre.dtype)
    return hn * jax.nn.sigmoid(o_pre)


def conv_glu(x, w_up, conv_w, conv_b, w_down):
    u = x @ w_up
    gate, val = jnp.split(u, 2, axis=-1)
    a = jax.nn.gelu(dwconv_centred(gate, conv_w, conv_b), approximate=False)
    return (a * val) @ w_down


def setup_inputs(seed: int = 0) -> dict:
    key = jax.random.key(seed)
    ks = jax.random.split(key, 20)
    f32 = jnp.float32
    nrm = lambda k, shape, sc: jax.random.normal(k, shape, f32) * sc
    L = DEPTH
    return {
        "x": nrm(ks[0], (BATCH, SEQ, D_MODEL), 1.0),
        "w_in": nrm(ks[1], (L, D_MODEL, IN_WIDTH), D_MODEL ** -0.5),
        "na_rpb": nrm(ks[2], (L, NA_HEADS, 2 * NA_WIN_ROWS - 1, 2 * NA_WIN_COLS - 1), 0.1),
        "ml_conv_w": nrm(ks[3], (L, CONV_WIDTH, 2 * ML_WIDTH), CONV_WIDTH ** -0.5),
        "ml_conv_b": nrm(ks[4], (L, 2 * ML_WIDTH), 0.02),
        "ml_igate_b": nrm(ks[5], (L, 2 * ML_HEADS), 0.1),
        "ml_fgate_b": jnp.linspace(3.0, 6.0, 2 * ML_HEADS, dtype=f32)[None, :] + nrm(ks[6], (L, 2 * ML_HEADS), 0.1),
        "ml_norm_w": 1.0 + nrm(ks[7], (L, ML_WIDTH), 0.1),
        "w_branch_na": nrm(ks[8], (L, NA_WIDTH, D_MODEL), NA_WIDTH ** -0.5),
        "w_branch_ml": nrm(ks[9], (L, ML_WIDTH, D_MODEL), ML_WIDTH ** -0.5),
        "w_out": nrm(ks[10], (L, D_MODEL, D_MODEL), D_MODEL ** -0.5 * DEEPNORM_BETA),
        "ln1_g": 1.0 + nrm(ks[11], (L, D_MODEL), 0.1),
        "ln1_b": nrm(ks[12], (L, D_MODEL), 0.02),
        "ffn_w_up": nrm(ks[13], (L, D_MODEL, 2 * D_FF), D_MODEL ** -0.5),
        "ffn_conv_w": nrm(ks[14], (L, CONV_WIDTH, D_FF), CONV_WIDTH ** -0.5),
        "ffn_conv_b": nrm(ks[15], (L, D_FF), 0.02),
        "ffn_w_down": nrm(ks[16], (L, D_FF, D_MODEL), D_FF ** -0.5 * DEEPNORM_BETA),
        "ln2_g": 1.0 + nrm(ks[17], (L, D_MODEL), 0.1),
        "ln2_b": nrm(ks[18], (L, D_MODEL), 0.02),
    }


def reference(x, w_in, na_rpb, ml_conv_w, ml_conv_b, ml_igate_b, ml_fgate_b, ml_norm_w,
              w_branch_na, w_branch_ml, w_out, ln1_g, ln1_b, ffn_w_up, ffn_conv_w, ffn_conv_b,
              ffn_w_down, ln2_g, ln2_b):
    bsz, s, _ = x.shape
    for l in range(DEPTH):
        hproj = x @ w_in[l]
        na_qkv, ml_q, ml_k, ml_v, ml_o, ml_gates, merge_g = jnp.split(hproj, IN_SPLIT_POINTS, axis=-1)
        na_qkv = na_qkv.reshape(bsz, s, 3, NA_HEADS, NA_HEAD_DIM)
        y_na = neighborhood_attention(na_qkv[:, :, 0], na_qkv[:, :, 1], na_qkv[:, :, 2], na_rpb[l])
        y_ml = mlstm_branch(ml_q, ml_k, ml_v, ml_o, ml_gates, ml_conv_w[l], ml_conv_b[l],
                            ml_igate_b[l], ml_fgate_b[l], ml_norm_w[l])
        g_na, g_ml = jnp.split(jax.nn.sigmoid(merge_g), 2, axis=-1)
        mixed = g_na * (y_na @ w_branch_na[l]) + g_ml * (y_ml @ w_branch_ml[l])
        x = layer_norm(DEEPNORM_ALPHA * x + mixed @ w_out[l], ln1_g[l], ln1_b[l])
        f = conv_glu(x, ffn_w_up[l], ffn_conv_w[l], ffn_conv_b[l], ffn_w_down[l])
        x = layer_norm(DEEPNORM_ALPHA * x + f, ln2_g[l], ln2_b[l])
    return x
```

```python
import functools
import math

import numpy as np
import jax
import jax.numpy as jnp
from jax import lax
from jax.experimental import pallas as pl
from jax.experimental.pallas import tpu as pltpu

GRID_W = 64
NA_HEADS = 16
NA_HEAD_DIM = 64
NA_WIDTH = NA_HEADS * NA_HEAD_DIM
NA_WIN_ROWS = 8
NA_WIN_COLS = 16
ML_HEADS = 4
ML_HEAD_DIM = 256
ML_WIDTH = ML_HEADS * ML_HEAD_DIM
ML_CHUNK = 128
LN_EPS = 1e-5
NEG_BIG = -1e30

LANES = 128
BF16_ROWS = 16
VMEM_LIMIT = 56 * 1024 * 1024

F32 = jnp.float32
BF16 = jnp.bfloat16

COL_MERGE_NA = 0
COL_MERGE_ML = 2048
COL_NA_Q = 4096
COL_NA_K = 5120
COL_NA_V = 6144
COL_ML_Q = 7168
COL_ML_K = 8192
COL_ML_V = 9216
COL_ML_O = 10240
HP_WIDTH = 11264


def _dot(a, b):
    return jnp.dot(a, b, preferred_element_type=F32)


def _dot_nt(a, b):
    return lax.dot_general(a, b, (((1,), (1,)), ((), ())), preferred_element_type=F32)


def _dot_tn(a, b):
    return lax.dot_general(a, b, (((0,), (0,)), ((), ())), preferred_element_type=F32)


def _layer_norm(z, g, b):
    mu = jnp.mean(z, axis=-1, keepdims=True)
    zc = z - mu
    var = jnp.mean(zc * zc, axis=-1, keepdims=True)
    return zc * lax.rsqrt(var + LN_EPS) * g + b


def _in_proj_kernel(x_ref, w_ref, wg_ref, o_ref, g_ref, xb_ref):
    @pl.when(pl.program_id(1) == 0)
    def _():
        xb = x_ref[...].astype(BF16)
        xb_ref[...] = xb
        g_ref[...] = _dot(xb, wg_ref[...])

    o_ref[...] = _dot(xb_ref[...], w_ref[...]).astype(o_ref.dtype)


def _in_proj(x2, w_main, w_gate):
    m, d = x2.shape
    n = w_main.shape[1]
    tm = min(1024, m)
    tn = 1024
    return pl.pallas_call(
        _in_proj_kernel,
        grid=(m // tm, n // tn),
        in_specs=[
            pl.BlockSpec((tm, d), lambda i, j: (i, 0)),
            pl.BlockSpec((d, tn), lambda i, j: (0, j)),
            pl.BlockSpec((d, LANES), lambda i, j: (0, 0)),
        ],
        out_specs=[
            pl.BlockSpec((tm, tn), lambda i, j: (i, j)),
            pl.BlockSpec((tm, LANES), lambda i, j: (i, 0)),
        ],
        out_shape=[
            jax.ShapeDtypeStruct((m, n), BF16),
            jax.ShapeDtypeStruct((m, LANES), F32),
        ],
        scratch_shapes=[pltpu.VMEM((tm, d), BF16)],
        compiler_params=pltpu.CompilerParams(
            dimension_semantics=("parallel", "arbitrary"), vmem_limit_bytes=VMEM_LIMIT),
        name="in_proj",
    )(x2, w_main, w_gate)


def _na_kernel(q_ref, k_ref, v_ref, bias_ref, o_ref, *, rows):
    kh = NA_WIN_ROWS
    lane = lax.broadcasted_iota(jnp.int32, (GRID_W, LANES), 1)
    qcol = lax.broadcasted_iota(jnp.int32, (GRID_W, LANES), 0)
    kcol = lane & (GRID_W - 1)
    wstart = jnp.clip(qcol - NA_WIN_COLS // 2, 0, GRID_W - NA_WIN_COLS)
    valid = (kcol >= wstart) & (kcol < wstart + NA_WIN_COLS)
    head_a = lane < NA_HEAD_DIM
    scale = NA_HEAD_DIM ** -0.5

    def row_step(r, carry):
        rs = jnp.clip(r - kh // 2, 0, rows - kh)
        dr0 = rs - r + (NA_WIN_ROWS - 1)
        q = q_ref[pl.ds(pl.multiple_of(r * GRID_W, GRID_W), GRID_W), :] * scale
        zero = jnp.zeros_like(q)
        kbase = pl.multiple_of(rs * GRID_W, GRID_W)
        outs = []
        for h, qh in enumerate((jnp.where(head_a, q, zero), jnp.where(head_a, zero, q))):
            s = []
            for c in range(kh // 2):
                kc = k_ref[pl.ds(kbase + c * LANES, LANES), :]
                sc = _dot_nt(qh, kc) + bias_ref[h, dr0 + 2 * c]
                s.append(jnp.where(valid, sc, NEG_BIG))
            m = jnp.maximum(jnp.maximum(s[0], s[1]), jnp.maximum(s[2], s[3]))
            m = jnp.max(m, axis=-1, keepdims=True)
            acc = jnp.zeros((GRID_W, LANES), F32)
            l = jnp.zeros((GRID_W, 1), F32)
            for c in range(kh // 2):
                p = jnp.exp(s[c] - m)
                l = l + jnp.sum(p, axis=-1, keepdims=True)
                vc = v_ref[pl.ds(kbase + c * LANES, LANES), :]
                acc = acc + _dot(p.astype(BF16), vc)
            outs.append(acc / l)
        o = jnp.where(head_a, outs[0], outs[1])
        o_ref[pl.ds(pl.multiple_of(r * GRID_W, GRID_W), GRID_W), :] = o.astype(o_ref.dtype)
        return carry

    lax.fori_loop(0, rows, row_step, 0)


def _na_bias_table(rpb):
    qc = np.arange(GRID_W)[:, None]
    kc = np.arange(GRID_W)[None, :]
    dc = np.clip(kc - qc + NA_WIN_COLS - 1, 0, 2 * NA_WIN_COLS - 2)
    b = rpb[:, :, dc]
    t = jnp.concatenate([b[:, :-1], b[:, 1:]], axis=-1)
    return t.reshape(NA_HEADS // 2, 2, 2 * NA_WIN_ROWS - 2, GRID_W, LANES)


def _na_attention(hp3, rpb):
    bsz, s, _ = hp3.shape
    rows = s // GRID_W
    assert s % GRID_W == 0 and rows >= NA_WIN_ROWS
    table = _na_bias_table(rpb.astype(F32))
    qb, kb, vb = COL_NA_Q // LANES, COL_NA_K // LANES, COL_NA_V // LANES
    return pl.pallas_call(
        functools.partial(_na_kernel, rows=rows),
        grid=(bsz, NA_HEADS // 2),
        in_specs=[
            pl.BlockSpec((None, s, LANES), lambda b, h: (b, 0, qb + h)),
            pl.BlockSpec((None, s, LANES), lambda b, h: (b, 0, kb + h)),
            pl.BlockSpec((None, s, LANES), lambda b, h: (b, 0, vb + h)),
            pl.BlockSpec((None, 2, 2 * NA_WIN_ROWS - 2, GRID_W, LANES), lambda b, h: (h, 0, 0, 0, 0)),
        ],
        out_specs=pl.BlockSpec((None, s, LANES), lambda b, h: (b, 0, h)),
        out_shape=jax.ShapeDtypeStruct((bsz, s, NA_WIDTH), BF16),
        compiler_params=pltpu.CompilerParams(
            dimension_semantics=("parallel", "parallel"), vmem_limit_bytes=VMEM_LIMIT),
        name="na_attn",
    )(hp3, hp3, hp3, table)


def _log_sigmoid(x):
    return jnp.minimum(x, 0.0) - jnp.log1p(jnp.exp(-jnp.abs(x)))


def _mlstm_kernel(q_ref, k_ref, v_ref, o_ref, grow_ref, gcol_ref, brow_ref, bcol_ref,
                  cwq_ref, cwk_ref, cbq_ref, cbk_ref, nw_ref, y_ref,
                  qs_ref, ks_ref, hf_ref, hb_ref, c_ref, *, seq):
    L = ML_CHUNK
    nc = seq // L
    row = lax.broadcasted_iota(jnp.int32, (L, L), 0)
    col = lax.broadcasted_iota(jnp.int32, (L, L), 1)
    tril = row >= col
    triu = row <= col
    rowid = lax.broadcasted_iota(jnp.int32, (L, ML_HEAD_DIM), 0)

    def conv_step(c, carry):
        t0 = pl.multiple_of(c * L, L)
        tp = pl.multiple_of(jnp.maximum(t0 - BF16_ROWS, 0), BF16_ROWS)
        tn = pl.multiple_of(jnp.minimum(t0 + L, seq - BF16_ROWS), BF16_ROWS)
        has_prev = (c > 0).astype(F32)
        has_next = (c < nc - 1).astype(F32)
        for src, dst, cw, cb, post in ((q_ref, qs_ref, cwq_ref, cbq_ref, 1.0),
                                       (k_ref, ks_ref, cwk_ref, cbk_ref, ML_HEAD_DIM ** -0.5)):
            x = src[pl.ds(t0, L), :].astype(F32)
            prev = src[pl.ds(tp, BF16_ROWS), :].astype(F32)[BF16_ROWS - 1:BF16_ROWS] * has_prev
            nxt = src[pl.ds(tn, BF16_ROWS), :].astype(F32)[0:1] * has_next
            xm1 = jnp.where(rowid == 0, prev, pltpu.roll(x, 1, 0))
            xp1 = jnp.where(rowid == L - 1, nxt, pltpu.roll(x, L - 1, 0))
            y = xm1 * cw[0:1, :]
            y = y + x * cw[1:2, :]
            y = y + xp1 * cw[2:3, :]
            y = y + cb[...]
            y = y * jax.nn.sigmoid(y)
            dst[pl.ds(t0, L), :] = (y * post).astype(BF16)
        return carry

    lax.fori_loop(0, nc, conv_step, 0)

    c_ref[...] = jnp.zeros_like(c_ref)

    def chunk(d, t0, m_st, n_st, h_ref):
        valid, valid_t = (tril, triu) if d == 0 else (triu, tril)
        q = qs_ref[pl.ds(t0, L), :]
        k = ks_ref[pl.ds(t0, L), :]
        v = v_ref[pl.ds(t0, L), :]
        li_row = grow_ref[d:d + 1, pl.ds(t0, L)] + brow_ref[d:d + 1, :]
        lf_row = _log_sigmoid(grow_ref[2 + d:3 + d, pl.ds(t0, L)] + brow_ref[2 + d:3 + d, :])
        li_col = gcol_ref[pl.ds(t0, L), d:d + 1] + bcol_ref[:, d:d + 1]
        lf_col = _log_sigmoid(gcol_ref[pl.ds(t0, L), 2 + d:3 + d] + bcol_ref[:, 2 + d:3 + d])
        b_col = jnp.sum(jnp.where(valid, lf_row, 0.0), axis=1, keepdims=True)
        b_row = jnp.sum(jnp.where(valid_t, lf_col, 0.0), axis=0, keepdims=True)
        b_last = jnp.sum(lf_row, axis=1, keepdims=True)
        dlog = jnp.where(valid, b_col - b_row + li_row, NEG_BIG)
        inter = b_col + m_st
        m_t = jnp.maximum(jnp.max(dlog, axis=1, keepdims=True), inter)
        dw = jnp.exp(dlog - m_t)
        w_inter = jnp.exp(inter - m_t)
        sqk = _dot_nt(q, k) * dw
        c_st = c_ref[d]
        num = w_inter * _dot(q, c_st.astype(BF16)) + _dot(sqk.astype(BF16), v)
        qn = jnp.sum(q.astype(F32) * n_st, axis=1, keepdims=True)
        den = w_inter * qn + jnp.sum(sqk, axis=1, keepdims=True)
        hc = num / jnp.maximum(jnp.abs(den), jnp.exp(-m_t))
        h_ref[pl.ds(t0, L), :] = hc
        logw = b_last - b_col + li_col
        m_new = jnp.maximum(b_last + m_st, jnp.max(logw, axis=0, keepdims=True))
        ws = jnp.exp(logw - m_new)
        decay = jnp.exp(b_last + m_st - m_new)
        kw = k.astype(F32) * ws
        c_ref[d] = decay * c_st + _dot_tn(kw.astype(BF16), v)
        n_new = decay * n_st + jnp.sum(kw, axis=0, keepdims=True)
        return m_new, n_new

    def rec_step(i, carry):
        m_f, n_f, m_b, n_b = carry
        m_f, n_f = chunk(0, pl.multiple_of(i * L, L), m_f, n_f, hf_ref)
        m_b, n_b = chunk(1, pl.multiple_of((nc - 1 - i) * L, L), m_b, n_b, hb_ref)
        return m_f, n_f, m_b, n_b

    m0 = jnp.full((1, 1), NEG_BIG, F32)
    n0 = jnp.zeros((1, ML_HEAD_DIM), F32)
    lax.fori_loop(0, nc, rec_step, (m0, n0, m0, n0))

    def out_step(c, carry):
        t0 = pl.multiple_of(c * L, L)
        hs = hf_ref[pl.ds(t0, L), :] + hb_ref[pl.ds(t0, L), :]
        mu = jnp.mean(hs, axis=-1, keepdims=True)
        hc = hs - mu
        var = jnp.mean(hc * hc, axis=-1, keepdims=True)
        hn = hc * lax.rsqrt(var + LN_EPS) * nw_ref[...]
        og = jax.nn.sigmoid(o_ref[pl.ds(t0, L), :].astype(F32))
        y_ref[pl.ds(t0, L), :] = (hn * og).astype(y_ref.dtype)
        return carry

    lax.fori_loop(0, nc, out_step, 0)


def _mlstm_branch(hp3, gates3, conv_w, conv_b, igate_b, fgate_b, norm_w):
    bsz, s, _ = hp3.shape
    assert s % ML_CHUNK == 0
    hd = ML_HEAD_DIM
    g = gates3[..., :4 * ML_HEADS].reshape(bsz, s, 4, ML_HEADS)
    gcol = g.transpose(0, 3, 1, 2)
    grow = g.transpose(0, 3, 2, 1)
    gb = jnp.concatenate([igate_b, fgate_b]).astype(F32).reshape(4, ML_HEADS).T
    brow = gb.reshape(ML_HEADS, 4, 1)
    bcol = gb.reshape(ML_HEADS, 1, 4)
    cw = conv_w.astype(F32)
    cb = conv_b.astype(F32).reshape(1, 2 * ML_WIDTH)
    nw = norm_w.astype(F32).reshape(1, ML_WIDTH)
    qb, kb, vb, ob = COL_ML_Q // hd, COL_ML_K // hd, COL_ML_V // hd, COL_ML_O // hd
    return pl.pallas_call(
        functools.partial(_mlstm_kernel, seq=s),
        grid=(bsz, ML_HEADS),
        in_specs=[
            pl.BlockSpec((None, s, hd), lambda b, h: (b, 0, qb + h)),
            pl.BlockSpec((None, s, hd), lambda b, h: (b, 0, kb + h)),
            pl.BlockSpec((None, s, hd), lambda b, h: (b, 0, vb + h)),
            pl.BlockSpec((None, s, hd), lambda b, h: (b, 0, ob + h)),
            pl.BlockSpec((None, None, 4, s), lambda b, h: (b, h, 0, 0)),
            pl.BlockSpec((None, None, s, 4), lambda b, h: (b, h, 0, 0)),
            pl.BlockSpec((None, 4, 1), lambda b, h: (h, 0, 0)),
            pl.BlockSpec((None, 1, 4), lambda b, h: (h, 0, 0)),
            pl.BlockSpec((3, hd), lambda b, h: (0, h)),
            pl.BlockSpec((3, hd), lambda b, h: (0, ML_HEADS + h)),
            pl.BlockSpec((1, hd), lambda b, h: (0, h)),
            pl.BlockSpec((1, hd), lambda b, h: (0, ML_HEADS + h)),
            pl.BlockSpec((1, hd), lambda b, h: (0, h)),
        ],
        out_specs=pl.BlockSpec((None, s, hd), lambda b, h: (b, 0, h)),
        out_shape=jax.ShapeDtypeStruct((bsz, s, ML_WIDTH), BF16),
        scratch_shapes=[
            pltpu.VMEM((s, hd), BF16),
            pltpu.VMEM((s, hd), BF16),
            pltpu.VMEM((s, hd), F32),
            pltpu.VMEM((s, hd), F32),
            pltpu.VMEM((2, hd, hd), F32),
        ],
        compiler_params=pltpu.CompilerParams(
            dimension_semantics=("parallel", "parallel"), vmem_limit_bytes=VMEM_LIMIT),
        name="mlstm",
    )(hp3, hp3, hp3, hp3, grow, gcol, brow, bcol, cw, cw, cb, cb, nw)


def _mix_kernel(x_ref, yna_ref, yml_ref, gna_ref, gml_ref, wna_ref, wml_ref, wo_ref, g_ref, b_ref,
                o_ref, *, alpha):
    a = _dot(yna_ref[...], wna_ref[...])
    mixed = jax.nn.sigmoid(gna_ref[...].astype(F32)) * a
    a = _dot(yml_ref[...], wml_ref[...])
    mixed = mixed + jax.nn.sigmoid(gml_ref[...].astype(F32)) * a
    z = alpha * x_ref[...] + _dot(mixed.astype(BF16), wo_ref[...])
    o_ref[...] = _layer_norm(z, g_ref[...], b_ref[...])


def _mix(x2, yna2, yml2, hp, wna, wml, wo, g, b, alpha):
    m, d = x2.shape
    tm = min(256, m)
    const = dict(pipeline_mode=pl.Buffered(1))
    return pl.pallas_call(
        functools.partial(_mix_kernel, alpha=alpha),
        grid=(m // tm,),
        in_specs=[
            pl.BlockSpec((tm, d), lambda i: (i, 0)),
            pl.BlockSpec((tm, NA_WIDTH), lambda i: (i, 0)),
            pl.BlockSpec((tm, ML_WIDTH), lambda i: (i, 0)),
            pl.BlockSpec((tm, d), lambda i: (i, COL_MERGE_NA // d)),
            pl.BlockSpec((tm, d), lambda i: (i, COL_MERGE_ML // d)),
            pl.BlockSpec((NA_WIDTH, d), lambda i: (0, 0), **const),
            pl.BlockSpec((ML_WIDTH, d), lambda i: (0, 0), **const),
            pl.BlockSpec((d, d), lambda i: (0, 0), **const),
            pl.BlockSpec((1, d), lambda i: (0, 0)),
            pl.BlockSpec((1, d), lambda i: (0, 0)),
        ],
        out_specs=pl.BlockSpec((tm, d), lambda i: (i, 0)),
        out_shape=jax.ShapeDtypeStruct((m, d), F32),
        compiler_params=pltpu.CompilerParams(
            dimension_semantics=("parallel",), vmem_limit_bytes=VMEM_LIMIT),
        name="mix",
    )(x2, yna2, yml2, hp, hp, wna, wml, wo, g, b)


FFN_HALO = BF16_ROWS


def _gelu_exact(x):
    return 0.5 * x * (1.0 + lax.erf(x * np.float32(math.sqrt(0.5))))


def _ffn_kernel(x_ref, xp_ref, xn_ref, wg_ref, wv_ref, wd_ref, cw_ref, cb_ref, g_ref, b_ref,
                o_ref, xb_ref, acc_ref, *, alpha, seq, tm):
    i = pl.program_id(0)
    j = pl.program_id(1)
    hl = FFN_HALO

    @pl.when(j == 0)
    def _():
        has_prev = ((i * tm) % seq != 0).astype(F32)
        has_next = (((i + 1) * tm) % seq != 0).astype(F32)
        xb_ref[0:hl, :] = (xp_ref[...] * has_prev).astype(BF16)
        xb_ref[hl:hl + tm, :] = x_ref[...].astype(BF16)
        xb_ref[hl + tm:, :] = (xn_ref[...] * has_next).astype(BF16)
        acc_ref[...] = jnp.zeros_like(acc_ref)

    ug = _dot(xb_ref[...], wg_ref[...])
    uv = _dot(xb_ref[hl:hl + tm, :], wv_ref[...])
    rows = tm + 2 * hl
    um1 = pltpu.roll(ug, 1, 0)[hl:hl + tm]
    up1 = pltpu.roll(ug, rows - 1, 0)[hl:hl + tm]
    c = um1 * cw_ref[0:1, :]
    c = c + ug[hl:hl + tm] * cw_ref[1:2, :]
    c = c + up1 * cw_ref[2:3, :]
    c = c + cb_ref[...]
    h = (_gelu_exact(c) * uv).astype(BF16)
    acc_ref[...] += _dot(h, wd_ref[...])

    @pl.when(j == pl.num_programs(1) - 1)
    def _():
        z = alpha * x_ref[...] + acc_ref[...]
        o_ref[...] = _layer_norm(z, g_ref[...], b_ref[...])


def _ffn(x1, wg, wv, wd, cw, cb, g, b, alpha, seq):
    m, d = x1.shape
    dffp = wg.shape[1]
    tm = min(512, seq)
    tf = 512
    hl = FFN_HALO
    assert seq % tm == 0 and m % tm == 0 and dffp % tf == 0
    nb = tm // hl
    last = m // hl - 1
    return pl.pallas_call(
        functools.partial(_ffn_kernel, alpha=alpha, seq=seq, tm=tm),
        grid=(m // tm, dffp // tf),
        in_specs=[
            pl.BlockSpec((tm, d), lambda i, j: (i, 0)),
            pl.BlockSpec((hl, d), lambda i, j: (jnp.maximum(i * nb - 1, 0), 0)),
            pl.BlockSpec((hl, d), lambda i, j: (jnp.minimum((i + 1) * nb, last), 0)),
            pl.BlockSpec((d, tf), lambda i, j: (0, j)),
            pl.BlockSpec((d, tf), lambda i, j: (0, j)),
            pl.BlockSpec((tf, d), lambda i, j: (j, 0)),
            pl.BlockSpec((3, tf), lambda i, j: (0, j)),
            pl.BlockSpec((1, tf), lambda i, j: (0, j)),
            pl.BlockSpec((1, d), lambda i, j: (0, 0)),
            pl.BlockSpec((1, d), lambda i, j: (0, 0)),
        ],
        out_specs=pl.BlockSpec((tm, d), lambda i, j: (i, 0)),
        out_shape=jax.ShapeDtypeStruct((m, d), F32),
        scratch_shapes=[
            pltpu.VMEM((tm + 2 * hl, d), BF16),
            pltpu.VMEM((tm, d), F32),
        ],
        compiler_params=pltpu.CompilerParams(
            dimension_semantics=("parallel", "arbitrary"), vmem_limit_bytes=VMEM_LIMIT),
        name="ffn",
    )(x1, x1, x1, wg, wv, wd, cw, cb, g, b)


def _pad_cols(a, n):
    return jnp.pad(a, ((0, 0), (0, n - a.shape[1])))


def kernel(x, w_in, na_rpb, ml_conv_w, ml_conv_b, ml_igate_b, ml_fgate_b, ml_norm_w, w_branch_na,
           w_branch_ml, w_out, ln1_g, ln1_b, ffn_w_up, ffn_conv_w, ffn_conv_b, ffn_w_down, ln2_g, ln2_b):
    bsz, s, d = x.shape
    depth = w_in.shape[0]
    alpha = float((2 * depth) ** 0.25)
    dff = ffn_conv_w.shape[-1]
    dffp = -(-dff // 512) * 512
    m = bsz * s
    na3 = 3 * NA_WIDTH
    gates_at = na3 + 4 * ML_WIDTH
    merge_at = gates_at + 4 * ML_HEADS
    x2 = x.reshape(m, d)
    for l in range(depth):
        w = w_in[l]
        w_main = jnp.concatenate([w[:, merge_at:], w[:, :gates_at]], axis=1).astype(BF16)
        w_gate = _pad_cols(w[:, gates_at:merge_at], LANES).astype(BF16)
        hp, gates = _in_proj(x2, w_main, w_gate)
        hp3 = hp.reshape(bsz, s, HP_WIDTH)
        y_na = _na_attention(hp3, na_rpb[l])
        y_ml = _mlstm_branch(hp3, gates.reshape(bsz, s, LANES), ml_conv_w[l], ml_conv_b[l],
                             ml_igate_b[l], ml_fgate_b[l], ml_norm_w[l])
        x1 = _mix(x2, y_na.reshape(m, NA_WIDTH), y_ml.reshape(m, ML_WIDTH), hp,
                  w_branch_na[l].astype(BF16), w_branch_ml[l].astype(BF16), w_out[l].astype(BF16),
                  ln1_g[l].reshape(1, d), ln1_b[l].reshape(1, d), alpha)
        wg = _pad_cols(ffn_w_up[l][:, :dff], dffp).astype(BF16)
        wv = _pad_cols(ffn_w_up[l][:, dff:], dffp).astype(BF16)
        wd = jnp.pad(ffn_w_down[l], ((0, dffp - dff), (0, 0))).astype(BF16)
        cw = _pad_cols(ffn_conv_w[l].astype(F32), dffp)
        cb = _pad_cols(ffn_conv_b[l].astype(F32).reshape(1, dff), dffp)
        x2 = _ffn(x1, wg, wv, wd, cw, cb, ln2_g[l].reshape(1, d), ln2_b[l].reshape(1, d), alpha, s)
    return x2.reshape(bsz, s, d)
```

```python
import functools
import math

import numpy as np
import jax
import jax.numpy as jnp
from jax import lax
from jax.experimental import pallas as pl
from jax.experimental.pallas import tpu as pltpu

GRID_W = 64
NA_HEADS = 16
NA_HEAD_DIM = 64
NA_WIDTH = NA_HEADS * NA_HEAD_DIM
NA_WIN_ROWS = 8
NA_WIN_COLS = 16
NA_ROW_GROUP = 8
ML_HEADS = 4
ML_HEAD_DIM = 256
ML_WIDTH = ML_HEADS * ML_HEAD_DIM
ML_CHUNK = 128
LN_EPS = 1e-5
NEG_BIG = -1e30

LANES = 128
BF16_ROWS = 16
VMEM_LIMIT = 56 * 1024 * 1024

F32 = jnp.float32
BF16 = jnp.bfloat16

COL_MERGE_NA = 0
COL_MERGE_ML = 2048
COL_NA_Q = 4096
COL_NA_K = 5120
COL_NA_V = 6144
COL_ML_Q = 7168
COL_ML_K = 8192
COL_ML_V = 9216
COL_ML_O = 10240
HP_WIDTH = 11264


def _dot(a, b):
    return jnp.dot(a, b, preferred_element_type=F32)


def _dot_nt(a, b):
    return lax.dot_general(a, b, (((1,), (1,)), ((), ())), preferred_element_type=F32)


def _dot_tn(a, b):
    return lax.dot_general(a, b, (((0,), (0,)), ((), ())), preferred_element_type=F32)


def _layer_norm(z, g, b):
    mu = jnp.mean(z, axis=-1, keepdims=True)
    zc = z - mu
    var = jnp.mean(zc * zc, axis=-1, keepdims=True)
    return zc * lax.rsqrt(var + LN_EPS) * g + b


def _in_proj_kernel(x_ref, w_ref, wg_ref, o_ref, g_ref, xb_ref):
    @pl.when(pl.program_id(1) == 0)
    def _():
        xb = x_ref[...].astype(BF16)
        xb_ref[...] = xb
        g_ref[...] = _dot(xb, wg_ref[...])

    o_ref[...] = _dot(xb_ref[...], w_ref[...]).astype(o_ref.dtype)


def _in_proj(x2, w_main, w_gate):
    m, d = x2.shape
    n = w_main.shape[1]
    tm = min(1024, m)
    tn = 1024
    return pl.pallas_call(
        _in_proj_kernel,
        grid=(m // tm, n // tn),
        in_specs=[
            pl.BlockSpec((tm, d), lambda i, j: (i, 0)),
            pl.BlockSpec((d, tn), lambda i, j: (0, j)),
            pl.BlockSpec((d, LANES), lambda i, j: (0, 0)),
        ],
        out_specs=[
            pl.BlockSpec((tm, tn), lambda i, j: (i, j)),
            pl.BlockSpec((tm, LANES), lambda i, j: (i, 0)),
        ],
        out_shape=[
            jax.ShapeDtypeStruct((m, n), BF16),
            jax.ShapeDtypeStruct((m, LANES), F32),
        ],
        scratch_shapes=[pltpu.VMEM((tm, d), BF16)],
        compiler_params=pltpu.CompilerParams(
            dimension_semantics=("parallel", "arbitrary"), vmem_limit_bytes=VMEM_LIMIT),
        name="in_proj",
    )(x2, w_main, w_gate)


def _na_kernel(q_ref, k_ref, v_ref, bias_ref, o_ref, sca_ref, scb_ref, *, rows, group):
    kh = NA_WIN_ROWS
    nkeys = kh * GRID_W
    lane = lax.broadcasted_iota(jnp.int32, (2 * GRID_W, LANES), 1)
    qcol = lax.broadcasted_iota(jnp.int32, (2 * GRID_W, LANES), 0) & (GRID_W - 1)
    kcol = lane & (GRID_W - 1)
    wstart = jnp.clip(qcol - NA_WIN_COLS // 2, 0, GRID_W - NA_WIN_COLS)
    valid = (kcol >= wstart) & (kcol < wstart + NA_WIN_COLS)
    head_a = lax.broadcasted_iota(jnp.int32, (GRID_W, LANES), 1) < NA_HEAD_DIM
    scale = NA_HEAD_DIM ** -0.5

    def window(r):
        rs = jnp.clip(r - kh // 2, 0, rows - kh)
        return pl.multiple_of(rs * GRID_W, GRID_W), rs - r + (NA_WIN_ROWS - 1)

    def score_stage(g, sc_ref):
        for u in range(group):
            r = g * group + u
            kbase, _ = window(r)
            q = q_ref[pl.ds(pl.multiple_of(r * GRID_W, GRID_W), GRID_W), :] * scale
            zero = jnp.zeros_like(q)
            qs = jnp.concatenate([jnp.where(head_a, q, zero), jnp.where(head_a, zero, q)], axis=0)
            sc_ref[u] = _dot_nt(qs, k_ref[pl.ds(kbase, nkeys), :])

    def value_stage(g, sc_ref):
        for u in range(group):
            r = g * group + u
            kbase, dr0 = window(r)
            s = []
            for c in range(kh // 2):
                sc_c = sc_ref[u, :, c * LANES:(c + 1) * LANES] + bias_ref[dr0 + 2 * c]
                s.append(jnp.where(valid, sc_c, NEG_BIG))
            m = jnp.maximum(jnp.maximum(s[0], s[1]), jnp.maximum(s[2], s[3]))
            m = jnp.max(m, axis=-1, keepdims=True)
            p = [jnp.exp(s_c - m) for s_c in s]
            l = jnp.sum((p[0] + p[1]) + (p[2] + p[3]), axis=-1, keepdims=True)
            pb = jnp.concatenate([p_c.astype(BF16) for p_c in p], axis=1)
            acc = _dot(pb, v_ref[pl.ds(kbase, nkeys), :]) / l
            o = jnp.where(head_a, acc[:GRID_W], acc[GRID_W:])
            o_ref[pl.ds(pl.multiple_of(r * GRID_W, GRID_W), GRID_W), :] = o.astype(o_ref.dtype)

    ngroups = rows // group
    score_stage(0, sca_ref)

    def pair_step(j, carry):
        score_stage(2 * j + 1, scb_ref)
        value_stage(2 * j, sca_ref)
        score_stage(2 * j + 2, sca_ref)
        value_stage(2 * j + 1, scb_ref)
        return carry

    lax.fori_loop(0, ngroups // 2 - 1, pair_step, 0)
    score_stage(ngroups - 1, scb_ref)
    value_stage(ngroups - 2, sca_ref)
    value_stage(ngroups - 1, scb_ref)


def _na_bias_table(rpb):
    qc = np.arange(GRID_W)[:, None]
    kc = np.arange(GRID_W)[None, :]
    dc = np.clip(kc - qc + NA_WIN_COLS - 1, 0, 2 * NA_WIN_COLS - 2)
    b = rpb[:, :, dc]
    t = jnp.concatenate([b[:, :-1], b[:, 1:]], axis=-1)
    t = t.reshape(NA_HEADS // 2, 2, 2 * NA_WIN_ROWS - 2, GRID_W, LANES)
    return t.transpose(0, 2, 1, 3, 4).reshape(NA_HEADS // 2, 2 * NA_WIN_ROWS - 2, 2 * GRID_W, LANES)


def _na_attention(hp3, rpb):
    bsz, s, _ = hp3.shape
    rows = s // GRID_W
    assert s % GRID_W == 0 and rows >= NA_WIN_ROWS
    table = _na_bias_table(rpb.astype(F32))
    qb, kb, vb = COL_NA_Q // LANES, COL_NA_K // LANES, COL_NA_V // LANES
    group = min(NA_ROW_GROUP, rows // 2)
    assert rows % (2 * group) == 0
    sc_buf = pltpu.VMEM((group, 2 * GRID_W, NA_WIN_ROWS * GRID_W), F32)
    return pl.pallas_call(
        functools.partial(_na_kernel, rows=rows, group=group),
        grid=(bsz, NA_HEADS // 2),
        in_specs=[
            pl.BlockSpec((None, s, LANES), lambda b, h: (b, 0, qb + h)),
            pl.BlockSpec((None, s, LANES), lambda b, h: (b, 0, kb + h)),
            pl.BlockSpec((None, s, LANES), lambda b, h: (b, 0, vb + h)),
            pl.BlockSpec((None, 2 * NA_WIN_ROWS - 2, 2 * GRID_W, LANES), lambda b, h: (h, 0, 0, 0)),
        ],
        out_specs=pl.BlockSpec((None, s, LANES), lambda b, h: (b, 0, h)),
        out_shape=jax.ShapeDtypeStruct((bsz, s, NA_WIDTH), BF16),
        scratch_shapes=[sc_buf, sc_buf],
        compiler_params=pltpu.CompilerParams(
            dimension_semantics=("parallel", "parallel"), vmem_limit_bytes=VMEM_LIMIT),
        name="na_attn",
    )(hp3, hp3, hp3, table)


def _log_sigmoid(x):
    return jnp.minimum(x, 0.0) - jnp.log1p(jnp.exp(-jnp.abs(x)))


def _mlstm_kernel(q_ref, k_ref, v_ref, o_ref, grow_ref, gcol_ref, brow_ref, bcol_ref,
                  cwq_ref, cwk_ref, cbq_ref, cbk_ref, nw_ref, y_ref,
                  qs_ref, ks_ref, hf_ref, hb_ref, c_ref, *, seq):
    L = ML_CHUNK
    nc = seq // L
    row = lax.broadcasted_iota(jnp.int32, (L, L), 0)
    col = lax.broadcasted_iota(jnp.int32, (L, L), 1)
    tril = row >= col
    triu = row <= col
    rowid = lax.broadcasted_iota(jnp.int32, (L, ML_HEAD_DIM), 0)

    def conv_step(c, carry):
        t0 = pl.multiple_of(c * L, L)
        tp = pl.multiple_of(jnp.maximum(t0 - BF16_ROWS, 0), BF16_ROWS)
        tn = pl.multiple_of(jnp.minimum(t0 + L, seq - BF16_ROWS), BF16_ROWS)
        has_prev = (c > 0).astype(F32)
        has_next = (c < nc - 1).astype(F32)
        for src, dst, cw, cb, post in ((q_ref, qs_ref, cwq_ref, cbq_ref, 1.0),
                                       (k_ref, ks_ref, cwk_ref, cbk_ref, ML_HEAD_DIM ** -0.5)):
            x = src[pl.ds(t0, L), :].astype(F32)
            prev = src[pl.ds(tp, BF16_ROWS), :].astype(F32)[BF16_ROWS - 1:BF16_ROWS] * has_prev
            nxt = src[pl.ds(tn, BF16_ROWS), :].astype(F32)[0:1] * has_next
            xm1 = jnp.where(rowid == 0, prev, pltpu.roll(x, 1, 0))
            xp1 = jnp.where(rowid == L - 1, nxt, pltpu.roll(x, L - 1, 0))
            y = xm1 * cw[0:1, :]
            y = y + x * cw[1:2, :]
            y = y + xp1 * cw[2:3, :]
            y = y + cb[...]
            y = y * jax.nn.sigmoid(y)
            dst[pl.ds(t0, L), :] = (y * post).astype(BF16)
        return carry

    lax.fori_loop(0, nc, conv_step, 0)

    c_ref[...] = jnp.zeros_like(c_ref)

    eye = row == col

    def local_scores(d, c):
        t0 = pl.multiple_of(c * L, L)
        q = qs_ref[pl.ds(t0, L), :]
        k = ks_ref[pl.ds(t0, L), :]
        v = v_ref[pl.ds(t0, L), :]
        return dict(d=d, t0=t0, q=q, k=k, v=v, s_qk=_dot_nt(q, k))

    def local_gates(ch):
        d, t0 = ch["d"], ch["t0"]
        valid = tril if d == 0 else triu
        li_row = grow_ref[d:d + 1, pl.ds(t0, L)] + brow_ref[d:d + 1, :]
        lf_row = _log_sigmoid(grow_ref[2 + d:3 + d, pl.ds(t0, L)] + brow_ref[2 + d:3 + d, :])
        li_col = gcol_ref[pl.ds(t0, L), d:d + 1] + bcol_ref[:, d:d + 1]
        b_col = jnp.sum(jnp.where(valid, lf_row, 0.0), axis=1, keepdims=True)
        b_row = jnp.sum(jnp.where(eye, b_col, 0.0), axis=0, keepdims=True)
        b_last = jnp.sum(lf_row, axis=1, keepdims=True)
        dlog = jnp.where(valid, b_col - b_row + li_row, NEG_BIG)
        m_loc = jnp.max(dlog, axis=1, keepdims=True)
        a_loc = ch["s_qk"] * jnp.exp(dlog - m_loc)
        logw = (b_last - b_col) + jnp.broadcast_to(li_col, (L, L))
        m_w = jnp.max(logw, axis=0, keepdims=True)
        ws = jnp.exp(logw - m_w)
        kf = ch["k"].astype(F32)
        kw = jnp.concatenate([kf[:, :L] * ws, kf[:, L:] * ws], axis=1)
        ch.update(b_col=b_col, b_last=b_last, m_loc=m_loc, a_loc=a_loc,
                  rowsum=jnp.sum(a_loc, axis=1, keepdims=True), m_w=m_w[:, 0:1], kw=kw,
                  ksum=jnp.sum(kw, axis=0, keepdims=True))

    def local_matmuls(ch):
        ch["h_loc"] = _dot(ch["a_loc"].astype(BF16), ch["v"])
        ch["kv"] = _dot_tn(ch["kw"].astype(BF16), ch["v"])

    def state_step(ch, m_st, n_st, h_ref):
        d = ch["d"]
        c_st = c_ref[d]
        q = ch["q"]
        qc = _dot(q, c_st.astype(BF16))
        qn = jnp.sum(q.astype(F32) * n_st, axis=1, keepdims=True)
        inter = ch["b_col"] + m_st
        m_t = jnp.maximum(ch["m_loc"], inter)
        w_inter = jnp.exp(inter - m_t)
        w_loc = jnp.exp(ch["m_loc"] - m_t)
        den = w_inter * qn + w_loc * ch["rowsum"]
        inv = 1.0 / jnp.maximum(jnp.abs(den), jnp.exp(-m_t))
        h_ref[pl.ds(ch["t0"], L), :] = (w_inter * inv) * qc + (w_loc * inv) * ch["h_loc"]
        m_new = jnp.maximum(ch["b_last"] + m_st, ch["m_w"])
        decay = jnp.exp(ch["b_last"] + m_st - m_new)
        scale = jnp.exp(ch["m_w"] - m_new)
        c_ref[d] = decay * c_st + scale * ch["kv"]
        return m_new, decay * n_st + scale * ch["ksum"]

    def rec_step(i, carry):
        m_f, n_f, m_b, n_b = carry
        chains = [local_scores(0, 2 * i), local_scores(1, nc - 1 - 2 * i),
                  local_scores(0, 2 * i + 1), local_scores(1, nc - 2 - 2 * i)]
        for ch in chains:
            local_gates(ch)
        for ch in chains:
            local_matmuls(ch)
        m_f, n_f = state_step(chains[0], m_f, n_f, hf_ref)
        m_b, n_b = state_step(chains[1], m_b, n_b, hb_ref)
        m_f, n_f = state_step(chains[2], m_f, n_f, hf_ref)
        m_b, n_b = state_step(chains[3], m_b, n_b, hb_ref)
        return m_f, n_f, m_b, n_b

    assert nc % 2 == 0
    m0 = jnp.full((1, 1), NEG_BIG, F32)
    n0 = jnp.zeros((1, ML_HEAD_DIM), F32)
    lax.fori_loop(0, nc // 2, rec_step, (m0, n0, m0, n0))

    def out_step(c, carry):
        t0 = pl.multiple_of(c * L, L)
        hs = hf_ref[pl.ds(t0, L), :] + hb_ref[pl.ds(t0, L), :]
        mu = jnp.mean(hs, axis=-1, keepdims=True)
        hc = hs - mu
        var = jnp.mean(hc * hc, axis=-1, keepdims=True)
        hn = hc * lax.rsqrt(var + LN_EPS) * nw_ref[...]
        og = jax.nn.sigmoid(o_ref[pl.ds(t0, L), :].astype(F32))
        y_ref[pl.ds(t0, L), :] = (hn * og).astype(y_ref.dtype)
        return carry

    lax.fori_loop(0, nc, out_step, 0, unroll=4)


def _mlstm_branch(hp3, gates3, conv_w, conv_b, igate_b, fgate_b, norm_w):
    bsz, s, _ = hp3.shape
    assert s % ML_CHUNK == 0
    hd = ML_HEAD_DIM
    g = gates3[..., :4 * ML_HEADS].reshape(bsz, s, 4, ML_HEADS)
    gcol = g.transpose(0, 3, 1, 2)
    grow = g.transpose(0, 3, 2, 1)
    gb = jnp.concatenate([igate_b, fgate_b]).astype(F32).reshape(4, ML_HEADS).T
    brow = gb.reshape(ML_HEADS, 4, 1)
    bcol = gb.reshape(ML_HEADS, 1, 4)
    cw = conv_w.astype(F32)
    cb = conv_b.astype(F32).reshape(1, 2 * ML_WIDTH)
    nw = norm_w.astype(F32).reshape(1, ML_WIDTH)
    qb, kb, vb, ob = COL_ML_Q // hd, COL_ML_K // hd, COL_ML_V // hd, COL_ML_O // hd
    return pl.pallas_call(
        functools.partial(_mlstm_kernel, seq=s),
        grid=(bsz, ML_HEADS),
        in_specs=[
            pl.BlockSpec((None, s, hd), lambda b, h: (b, 0, qb + h)),
            pl.BlockSpec((None, s, hd), lambda b, h: (b, 0, kb + h)),
            pl.BlockSpec((None, s, hd), lambda b, h: (b, 0, vb + h)),
            pl.BlockSpec((None, s, hd), lambda b, h: (b, 0, ob + h)),
            pl.BlockSpec((None, None, 4, s), lambda b, h: (b, h, 0, 0)),
            pl.BlockSpec((None, None, s, 4), lambda b, h: (b, h, 0, 0)),
            pl.BlockSpec((None, 4, 1), lambda b, h: (h, 0, 0)),
            pl.BlockSpec((None, 1, 4), lambda b, h: (h, 0, 0)),
            pl.BlockSpec((3, hd), lambda b, h: (0, h)),
            pl.BlockSpec((3, hd), lambda b, h: (0, ML_HEADS + h)),
            pl.BlockSpec((1, hd), lambda b, h: (0, h)),
            pl.BlockSpec((1, hd), lambda b, h: (0, ML_HEADS + h)),
            pl.BlockSpec((1, hd), lambda b, h: (0, h)),
        ],
        out_specs=pl.BlockSpec((None, s, hd), lambda b, h: (b, 0, h)),
        out_shape=jax.ShapeDtypeStruct((bsz, s, ML_WIDTH), BF16),
        scratch_shapes=[
            pltpu.VMEM((s, hd), BF16),
            pltpu.VMEM((s, hd), BF16),
            pltpu.VMEM((s, hd), F32),
            pltpu.VMEM((s, hd), F32),
            pltpu.VMEM((2, hd, hd), F32),
        ],
        compiler_params=pltpu.CompilerParams(
            dimension_semantics=("parallel", "parallel"), vmem_limit_bytes=VMEM_LIMIT),
        name="mlstm",
    )(hp3, hp3, hp3, hp3, grow, gcol, brow, bcol, cw, cw, cb, cb, nw)


def _mix_kernel(x_ref, yna_ref, yml_ref, gna_ref, gml_ref, wna_ref, wml_ref, wo_ref, g_ref, b_ref,
                o_ref, *, alpha):
    a = _dot(yna_ref[...], wna_ref[...])
    mixed = jax.nn.sigmoid(gna_ref[...].astype(F32)) * a
    a = _dot(yml_ref[...], wml_ref[...])
    mixed = mixed + jax.nn.sigmoid(gml_ref[...].astype(F32)) * a
    z = alpha * x_ref[...] + _dot(mixed.astype(BF16), wo_ref[...])
    o_ref[...] = _layer_norm(z, g_ref[...], b_ref[...])


def _mix(x2, yna2, yml2, hp, wna, wml, wo, g, b, alpha):
    m, d = x2.shape
    tm = min(256, m)
    const = dict(pipeline_mode=pl.Buffered(1))
    return pl.pallas_call(
        functools.partial(_mix_kernel, alpha=alpha),
        grid=(m // tm,),
        in_specs=[
            pl.BlockSpec((tm, d), lambda i: (i, 0)),
            pl.BlockSpec((tm, NA_WIDTH), lambda i: (i, 0)),
            pl.BlockSpec((tm, ML_WIDTH), lambda i: (i, 0)),
            pl.BlockSpec((tm, d), lambda i: (i, COL_MERGE_NA // d)),
            pl.BlockSpec((tm, d), lambda i: (i, COL_MERGE_ML // d)),
            pl.BlockSpec((NA_WIDTH, d), lambda i: (0, 0), **const),
            pl.BlockSpec((ML_WIDTH, d), lambda i: (0, 0), **const),
            pl.BlockSpec((d, d), lambda i: (0, 0), **const),
            pl.BlockSpec((1, d), lambda i: (0, 0)),
            pl.BlockSpec((1, d), lambda i: (0, 0)),
        ],
        out_specs=pl.BlockSpec((tm, d), lambda i: (i, 0)),
        out_shape=jax.ShapeDtypeStruct((m, d), F32),
        compiler_params=pltpu.CompilerParams(
            dimension_semantics=("parallel",), vmem_limit_bytes=VMEM_LIMIT),
        name="mix",
    )(x2, yna2, yml2, hp, hp, wna, wml, wo, g, b)


FFN_HALO = BF16_ROWS


def _gelu_exact(x):
    return 0.5 * x * (1.0 + lax.erf(x * np.float32(math.sqrt(0.5))))


def _ffn_kernel(x_ref, xp_ref, xn_ref, wg_ref, wv_ref, wd_ref, cw_ref, cb_ref, g_ref, b_ref,
                o_ref, xb_ref, acc_ref, *, alpha, seq, tm):
    i = pl.program_id(0)
    j = pl.program_id(1)
    hl = FFN_HALO

    @pl.when(j == 0)
    def _():
        has_prev = ((i * tm) % seq != 0).astype(F32)
        has_next = (((i + 1) * tm) % seq != 0).astype(F32)
        xb_ref[0:hl, :] = (xp_ref[...] * has_prev).astype(BF16)
        xb_ref[hl:hl + tm, :] = x_ref[...].astype(BF16)
        xb_ref[hl + tm:, :] = (xn_ref[...] * has_next).astype(BF16)
        acc_ref[...] = jnp.zeros_like(acc_ref)

    ug = _dot(xb_ref[...], wg_ref[...])
    uv = _dot(xb_ref[hl:hl + tm, :], wv_ref[...])
    rows = tm + 2 * hl
    um1 = pltpu.roll(ug, 1, 0)[hl:hl + tm]
    up1 = pltpu.roll(ug, rows - 1, 0)[hl:hl + tm]
    c = um1 * cw_ref[0:1, :]
    c = c + ug[hl:hl + tm] * cw_ref[1:2, :]
    c = c + up1 * cw_ref[2:3, :]
    c = c + cb_ref[...]
    h = (_gelu_exact(c) * uv).astype(BF16)
    acc_ref[...] += _dot(h, wd_ref[...])

    @pl.when(j == pl.num_programs(1) - 1)
    def _():
        z = alpha * x_ref[...] + acc_ref[...]
        o_ref[...] = _layer_norm(z, g_ref[...], b_ref[...])


def _ffn(x1, wg, wv, wd, cw, cb, g, b, alpha, seq):
    m, d = x1.shape
    dffp = wg.shape[1]
    tm = min(512, seq)
    tf = 512
    hl = FFN_HALO
    assert seq % tm == 0 and m % tm == 0 and dffp % tf == 0
    nb = tm // hl
    last = m // hl - 1
    return pl.pallas_call(
        functools.partial(_ffn_kernel, alpha=alpha, seq=seq, tm=tm),
        grid=(m // tm, dffp // tf),
        in_specs=[
            pl.BlockSpec((tm, d), lambda i, j: (i, 0)),
            pl.BlockSpec((hl, d), lambda i, j: (jnp.maximum(i * nb - 1, 0), 0)),
            pl.BlockSpec((hl, d), lambda i, j: (jnp.minimum((i + 1) * nb, last), 0)),
            pl.BlockSpec((d, tf), lambda i, j: (0, j)),
            pl.BlockSpec((d, tf), lambda i, j: (0, j)),
            pl.BlockSpec((tf, d), lambda i, j: (j, 0)),
            pl.BlockSpec((3, tf), lambda i, j: (0, j)),
            pl.BlockSpec((1, tf), lambda i, j: (0, j)),
            pl.BlockSpec((1, d), lambda i, j: (0, 0)),
            pl.BlockSpec((1, d), lambda i, j: (0, 0)),
        ],
        out_specs=pl.BlockSpec((tm, d), lambda i, j: (i, 0)),
        out_shape=jax.ShapeDtypeStruct((m, d), F32),
        scratch_shapes=[
            pltpu.VMEM((tm + 2 * hl, d), BF16),
            pltpu.VMEM((tm, d), F32),
        ],
        compiler_params=pltpu.CompilerParams(
            dimension_semantics=("parallel", "arbitrary"), vmem_limit_bytes=VMEM_LIMIT),
        name="ffn",
    )(x1, x1, x1, wg, wv, wd, cw, cb, g, b)


def _pad_cols(a, n):
    return jnp.pad(a, ((0, 0), (0, n - a.shape[1])))


def kernel(x, w_in, na_rpb, ml_conv_w, ml_conv_b, ml_igate_b, ml_fgate_b, ml_norm_w, w_branch_na,
           w_branch_ml, w_out, ln1_g, ln1_b, ffn_w_up, ffn_conv_w, ffn_conv_b, ffn_w_down, ln2_g, ln2_b):
    bsz, s, d = x.shape
    depth = w_in.shape[0]
    alpha = float((2 * depth) ** 0.25)
    dff = ffn_conv_w.shape[-1]
    dffp = -(-dff // 512) * 512
    m = bsz * s
    na3 = 3 * NA_WIDTH
    gates_at = na3 + 4 * ML_WIDTH
    merge_at = gates_at + 4 * ML_HEADS
    x2 = x.reshape(m, d)
    for l in range(depth):
        w = w_in[l]
        w_main = jnp.concatenate([w[:, merge_at:], w[:, :gates_at]], axis=1).astype(BF16)
        w_gate = _pad_cols(w[:, gates_at:merge_at], LANES).astype(BF16)
        hp, gates = _in_proj(x2, w_main, w_gate)
        hp3 = hp.reshape(bsz, s, HP_WIDTH)
        y_na = _na_attention(hp3, na_rpb[l])
        y_ml = _mlstm_branch(hp3, gates.reshape(bsz, s, LANES), ml_conv_w[l], ml_conv_b[l],
                             ml_igate_b[l], ml_fgate_b[l], ml_norm_w[l])
        x1 = _mix(x2, y_na.reshape(m, NA_WIDTH), y_ml.reshape(m, ML_WIDTH), hp,
                  w_branch_na[l].astype(BF16), w_branch_ml[l].astype(BF16), w_out[l].astype(BF16),
                  ln1_g[l].reshape(1, d), ln1_b[l].reshape(1, d), alpha)
        wg = _pad_cols(ffn_w_up[l][:, :dff], dffp).astype(BF16)
        wv = _pad_cols(ffn_w_up[l][:, dff:], dffp).astype(BF16)
        wd = jnp.pad(ffn_w_down[l], ((0, dffp - dff), (0, 0))).astype(BF16)
        cw = _pad_cols(ffn_conv_w[l].astype(F32), dffp)
        cb = _pad_cols(ffn_conv_b[l].astype(F32).reshape(1, dff), dffp)
        x2 = _ffn(x1, wg, wv, wd, cw, cb, ln2_g[l].reshape(1, d), ln2_b[l].reshape(1, d), alpha, s)
    return x2.reshape(bsz, s, d)
```

```python
import functools
import math

import numpy as np
import jax
import jax.numpy as jnp
from jax import lax
from jax.experimental import pallas as pl
from jax.experimental.pallas import tpu as pltpu

GRID_W = 64
NA_HEADS = 16
NA_HEAD_DIM = 64
NA_WIDTH = NA_HEADS * NA_HEAD_DIM
NA_WIN_ROWS = 8
NA_WIN_COLS = 16
NA_ROW_GROUP = 8
LOG2E = math.log2(math.e)
NA_Q_SCALE = NA_HEAD_DIM ** -0.5 * LOG2E
ML_HEADS = 4
ML_HEAD_DIM = 256
ML_WIDTH = ML_HEADS * ML_HEAD_DIM
ML_CHUNK = 128
LN_EPS = 1e-5
NEG_BIG = -1e30

LANES = 128
BF16_ROWS = 16
VMEM_LIMIT = 56 * 1024 * 1024

F32 = jnp.float32
BF16 = jnp.bfloat16

COL_MERGE_NA = 0
COL_MERGE_ML = 2048
COL_NA_Q = 4096
COL_NA_K = 5120
COL_NA_V = 6144
COL_ML_Q = 7168
COL_ML_K = 8192
COL_ML_V = 9216
COL_ML_O = 10240
HP_WIDTH = 11264


def _dot(a, b):
    return jnp.dot(a, b, preferred_element_type=F32)


def _dot_nt(a, b):
    return lax.dot_general(a, b, (((1,), (1,)), ((), ())), preferred_element_type=F32)


def _dot_tn(a, b):
    return lax.dot_general(a, b, (((0,), (0,)), ((), ())), preferred_element_type=F32)


def _layer_norm(z, g, b):
    mu = jnp.mean(z, axis=-1, keepdims=True)
    zc = z - mu
    var = jnp.mean(zc * zc, axis=-1, keepdims=True)
    return zc * lax.rsqrt(var + LN_EPS) * g + b


IN_TN = 1024
CONV_HALO = BF16_ROWS
CONV_COLS = 256


def _seq_edge_flags(i, tm, seq):
    has_prev = jnp.where((i * tm) % seq != 0, 1.0, 0.0).astype(F32)
    has_next = jnp.where(((i + 1) * tm) % seq != 0, 1.0, 0.0).astype(F32)
    return has_prev, has_next


def _fill_halo_tile(xb_ref, x_ref, xp_ref, xn_ref, i, tm, seq):
    hl = CONV_HALO
    has_prev, has_next = _seq_edge_flags(i, tm, seq)
    xb_ref[0:hl, :] = (xp_ref[...] * has_prev).astype(BF16)
    xb_ref[hl:hl + tm, :] = x_ref[...].astype(BF16)
    xb_ref[hl + tm:, :] = (xn_ref[...] * has_next).astype(BF16)


def _conv3_rows(u, u_ref, cw_ref, cb_ref, cols, tm):
    hl = CONV_HALO
    if u_ref is None:
        um1 = pltpu.roll(u, 1, 0)[hl:hl + tm]
        u0 = u[hl:hl + tm]
        up1 = pltpu.roll(u, u.shape[0] - 1, 0)[hl:hl + tm]
    else:
        u_ref[...] = u
        um1 = u_ref[hl - 1:hl - 1 + tm, :]
        u0 = u_ref[hl:hl + tm, :]
        up1 = u_ref[hl + 1:hl + 1 + tm, :]
    y = um1 * cw_ref[0:1, cols]
    y = y + u0 * cw_ref[1:2, cols]
    y = y + up1 * cw_ref[2:3, cols]
    return y + cb_ref[:, cols]


def _in_proj_kernel(x_ref, xp_ref, xn_ref, w_ref, wg_ref, cw_ref, cb_ref, o_ref, g_ref, xb_ref, ub_ref,
                    *, seq, tm, conv_blocks):
    i = pl.program_id(0)
    j = pl.program_id(1)
    hl = CONV_HALO
    q_block, k_block = conv_blocks

    @pl.when(j == 0)
    def _():
        _fill_halo_tile(xb_ref, x_ref, xp_ref, xn_ref, i, tm, seq)
        g_ref[...] = _dot(xb_ref[hl:hl + tm, :], wg_ref[...])

    is_conv = (j == q_block) | (j == k_block)

    @pl.when(jnp.logical_not(is_conv))
    def _():
        scale = jnp.where(j == COL_NA_Q // IN_TN, NA_Q_SCALE, 1.0).astype(F32)
        o_ref[...] = (_dot(xb_ref[hl:hl + tm, :], w_ref[...]) * scale).astype(o_ref.dtype)

    @pl.when(is_conv)
    def _():
        post = jnp.where(j == k_block, ML_HEAD_DIM ** -0.5, 1.0).astype(F32)
        for n, c0 in enumerate(range(0, IN_TN, CONV_COLS)):
            cols = slice(c0, c0 + CONV_COLS)
            u = _dot(xb_ref[...], w_ref[:, cols])
            y = _conv3_rows(u, ub_ref.at[n % 2], cw_ref, cb_ref, cols, tm)
            y = y * jax.nn.sigmoid(y)
            o_ref[:, cols] = (y * post).astype(o_ref.dtype)


def _in_proj(x2, w_main, w_gate, conv_w, conv_b, seq):
    m, d = x2.shape
    n = w_main.shape[1]
    tm = min(1024, seq)
    tn = IN_TN
    hl = CONV_HALO
    assert seq % tm == 0 and m % tm == 0 and COL_ML_Q % tn == 0 and COL_ML_K == COL_ML_Q + tn
    q_block = COL_ML_Q // tn
    nb = tm // hl
    last = m // hl - 1
    conv_idx = lambda i, j: (0, jnp.clip(j - q_block, 0, 1))
    return pl.pallas_call(
        functools.partial(_in_proj_kernel, seq=seq, tm=tm, conv_blocks=(q_block, q_block + 1)),
        grid=(m // tm, n // tn),
        in_specs=[
            pl.BlockSpec((tm, d), lambda i, j: (i, 0)),
            pl.BlockSpec((hl, d), lambda i, j: (jnp.maximum(i * nb - 1, 0), 0)),
            pl.BlockSpec((hl, d), lambda i, j: (jnp.minimum((i + 1) * nb, last), 0)),
            pl.BlockSpec((d, tn), lambda i, j: (0, j)),
            pl.BlockSpec((d, LANES), lambda i, j: (0, 0)),
            pl.BlockSpec((3, tn), conv_idx),
            pl.BlockSpec((1, tn), conv_idx),
        ],
        out_specs=[
            pl.BlockSpec((tm, tn), lambda i, j: (i, j)),
            pl.BlockSpec((tm, LANES), lambda i, j: (i, 0)),
        ],
        out_shape=[
            jax.ShapeDtypeStruct((m, n), BF16),
            jax.ShapeDtypeStruct((m, LANES), F32),
        ],
        scratch_shapes=[pltpu.VMEM((tm + 2 * hl, d), BF16),
                        pltpu.VMEM((2, tm + 2 * hl, CONV_COLS), F32)],
        compiler_params=pltpu.CompilerParams(
            dimension_semantics=("parallel", "arbitrary"), vmem_limit_bytes=VMEM_LIMIT),
        name="in_proj",
    )(x2, x2, x2, w_main, w_gate, conv_w, conv_b)


def _na_kernel(q_ref, k_ref, v_ref, bias_ref, o_ref, sca_ref, scb_ref, *, rows, group):
    kh = NA_WIN_ROWS
    nkeys = kh * GRID_W
    lane = lax.broadcasted_iota(jnp.int32, (2 * GRID_W, LANES), 1)
    qcol = lax.broadcasted_iota(jnp.int32, (2 * GRID_W, LANES), 0) & (GRID_W - 1)
    kcol = lane & (GRID_W - 1)
    wstart = jnp.clip(qcol - NA_WIN_COLS // 2, 0, GRID_W - NA_WIN_COLS)
    valid = (kcol >= wstart) & (kcol < wstart + NA_WIN_COLS)
    head_a = lax.broadcasted_iota(jnp.int32, (GRID_W, LANES), 1) < NA_HEAD_DIM

    def window(r):
        rs = jnp.clip(r - kh // 2, 0, rows - kh)
        return pl.multiple_of(rs * GRID_W, GRID_W), rs - r + (NA_WIN_ROWS - 1)

    def score_stage(g, sc_ref):
        for u in range(group):
            r = g * group + u
            kbase, _ = window(r)
            q = q_ref[pl.ds(pl.multiple_of(r * GRID_W, GRID_W), GRID_W), :]
            zero = jnp.zeros_like(q)
            qs = jnp.concatenate([jnp.where(head_a, q, zero), jnp.where(head_a, zero, q)], axis=0)
            sc_ref[u] = _dot_nt(qs, k_ref[pl.ds(kbase, nkeys), :])

    def value_stage(g, sc_ref):
        for u in range(group):
            r = g * group + u
            kbase, dr0 = window(r)
            s = []
            for c in range(kh // 2):
                sc_c = sc_ref[u, :, c * LANES:(c + 1) * LANES] + bias_ref[dr0 + 2 * c]
                s.append(jnp.where(valid, sc_c, NEG_BIG))
            m = jnp.maximum(jnp.maximum(s[0], s[1]), jnp.maximum(s[2], s[3]))
            m = jnp.max(m, axis=-1, keepdims=True)
            p = [jnp.exp2(s_c - m) for s_c in s]
            l = jnp.sum((p[0] + p[1]) + (p[2] + p[3]), axis=-1, keepdims=True)
            pb = jnp.concatenate([p_c.astype(BF16) for p_c in p], axis=1)
            acc = _dot(pb, v_ref[pl.ds(kbase, nkeys), :]) / l
            o = jnp.where(head_a, acc[:GRID_W], acc[GRID_W:])
            o_ref[pl.ds(pl.multiple_of(r * GRID_W, GRID_W), GRID_W), :] = o.astype(o_ref.dtype)

    ngroups = rows // group
    score_stage(0, sca_ref)

    def pair_step(j, carry):
        score_stage(2 * j + 1, scb_ref)
        value_stage(2 * j, sca_ref)
        score_stage(2 * j + 2, sca_ref)
        value_stage(2 * j + 1, scb_ref)
        return carry

    lax.fori_loop(0, ngroups // 2 - 1, pair_step, 0)
    score_stage(ngroups - 1, scb_ref)
    value_stage(ngroups - 2, sca_ref)
    value_stage(ngroups - 1, scb_ref)


def _na_bias_table(rpb):
    qc = np.arange(GRID_W)[:, None]
    kc = np.arange(GRID_W)[None, :]
    dc = np.clip(kc - qc + NA_WIN_COLS - 1, 0, 2 * NA_WIN_COLS - 2)
    b = rpb[:, :, dc]
    t = jnp.concatenate([b[:, :-1], b[:, 1:]], axis=-1)
    t = t.reshape(NA_HEADS // 2, 2, 2 * NA_WIN_ROWS - 2, GRID_W, LANES)
    return t.transpose(0, 2, 1, 3, 4).reshape(NA_HEADS // 2, 2 * NA_WIN_ROWS - 2, 2 * GRID_W, LANES)


def _na_attention(hp3, rpb):
    bsz, s, _ = hp3.shape
    rows = s // GRID_W
    assert s % GRID_W == 0 and rows >= NA_WIN_ROWS
    table = _na_bias_table(rpb.astype(F32) * LOG2E)
    qb, kb, vb = COL_NA_Q // LANES, COL_NA_K // LANES, COL_NA_V // LANES
    group = min(NA_ROW_GROUP, rows // 2)
    assert rows % (2 * group) == 0
    sc_buf = pltpu.VMEM((group, 2 * GRID_W, NA_WIN_ROWS * GRID_W), F32)
    return pl.pallas_call(
        functools.partial(_na_kernel, rows=rows, group=group),
        grid=(bsz, NA_HEADS // 2),
        in_specs=[
            pl.BlockSpec((None, s, LANES), lambda b, h: (b, 0, qb + h)),
            pl.BlockSpec((None, s, LANES), lambda b, h: (b, 0, kb + h)),
            pl.BlockSpec((None, s, LANES), lambda b, h: (b, 0, vb + h)),
            pl.BlockSpec((None, 2 * NA_WIN_ROWS - 2, 2 * GRID_W, LANES), lambda b, h: (h, 0, 0, 0)),
        ],
        out_specs=pl.BlockSpec((None, s, LANES), lambda b, h: (b, 0, h)),
        out_shape=jax.ShapeDtypeStruct((bsz, s, NA_WIDTH), BF16),
        scratch_shapes=[sc_buf, sc_buf],
        compiler_params=pltpu.CompilerParams(
            dimension_semantics=("parallel", "parallel"), vmem_limit_bytes=VMEM_LIMIT),
        name="na_attn",
    )(hp3, hp3, hp3, table)


def _log_sigmoid(x):
    return jnp.minimum(x, 0.0) - jnp.log1p(jnp.exp(-jnp.abs(x)))


def _mlstm_kernel(q_ref, k_ref, v_ref, o_ref, g_ref, gt_ref, gbl_ref, gbs_ref, nw_ref, y_ref,
                  hf_ref, hb_ref, c_ref, *, seq):
    L = ML_CHUNK
    nc = seq // L
    head = pl.program_id(1)
    row = lax.broadcasted_iota(jnp.int32, (L, L), 0)
    col = lax.broadcasted_iota(jnp.int32, (L, L), 1)
    tril = row >= col
    triu = row <= col
    ng = 2 * ML_HEADS
    sub8 = lax.broadcasted_iota(jnp.int32, (ng, L), 0)

    c_ref[...] = jnp.zeros_like(c_ref)

    eye = row == col

    def local_scores(d, c):
        t0 = pl.multiple_of(c * L, L)
        q = q_ref[pl.ds(t0, L), :]
        k = k_ref[pl.ds(t0, L), :]
        v = v_ref[pl.ds(t0, L), :]
        return dict(d=d, t0=t0, q=q, k=k, v=v, s_qk=_dot_nt(q, k))

    def local_gates(ch):
        d, t0 = ch["d"], ch["t0"]
        valid = tril if d == 0 else triu
        ich = d * ML_HEADS + head
        pick = sub8 == ich
        gi = gt_ref[0:ng, pl.ds(t0, L)] + gbs_ref[0:ng, :]
        gf = _log_sigmoid(gt_ref[ng:2 * ng, pl.ds(t0, L)] + gbs_ref[ng:2 * ng, :])
        li_row = jnp.sum(jnp.where(pick, gi, 0.0), axis=0, keepdims=True)
        lf_row = jnp.sum(jnp.where(pick, gf, 0.0), axis=0, keepdims=True)
        g_tile = g_ref[pl.ds(t0, L), :] + gbl_ref[...]
        li_col = jnp.sum(jnp.where(col == ich, g_tile, 0.0), axis=1, keepdims=True)
        b_col = jnp.sum(jnp.where(valid, lf_row, 0.0), axis=1, keepdims=True)
        b_row = jnp.sum(jnp.where(eye, b_col, 0.0), axis=0, keepdims=True)
        b_last = jnp.sum(lf_row, axis=1, keepdims=True)
        dlog = jnp.where(valid, b_col - b_row + li_row, NEG_BIG)
        m_loc = jnp.max(dlog, axis=1, keepdims=True)
        a_loc = ch["s_qk"] * jnp.exp(dlog - m_loc)
        logw = b_last - b_col + li_col
        m_w = jnp.max(logw, axis=0, keepdims=True)
        kw = ch["k"].astype(F32) * jnp.exp(logw - m_w)
        ch.update(b_col=b_col, b_last=b_last, m_loc=m_loc, a_loc=a_loc,
                  rowsum=jnp.sum(a_loc, axis=1, keepdims=True), m_w=m_w, kw=kw,
                  ksum=jnp.sum(kw, axis=0, keepdims=True))

    def local_matmuls(ch):
        ch["h_loc"] = _dot(ch["a_loc"].astype(BF16), ch["v"])
        ch["kv"] = _dot_tn(ch["kw"].astype(BF16), ch["v"])

    def state_step(ch, m_st, n_st, h_ref):
        d = ch["d"]
        c_st = c_ref[d]
        q = ch["q"]
        qc = _dot(q, c_st.astype(BF16))
        qn = jnp.sum(q.astype(F32) * n_st, axis=1, keepdims=True)
        inter = ch["b_col"] + m_st
        m_t = jnp.maximum(ch["m_loc"], inter)
        w_inter = jnp.exp(inter - m_t)
        w_loc = jnp.exp(ch["m_loc"] - m_t)
        den = w_inter * qn + w_loc * ch["rowsum"]
        inv = 1.0 / jnp.maximum(jnp.abs(den), jnp.exp(-m_t))
        h_ref[pl.ds(ch["t0"], L), :] = (w_inter * inv) * qc + (w_loc * inv) * ch["h_loc"]
        m_new = jnp.maximum(ch["b_last"] + m_st, ch["m_w"])
        decay = jnp.exp(ch["b_last"] + m_st - m_new)
        scale = jnp.exp(ch["m_w"] - m_new)
        c_ref[d] = decay * c_st + scale * ch["kv"]
        return m_new, decay * n_st + scale * ch["ksum"]

    def rec_step(i, carry):
        m_f, n_f, m_b, n_b = carry
        chains = [local_scores(0, 2 * i), local_scores(1, nc - 1 - 2 * i),
                  local_scores(0, 2 * i + 1), local_scores(1, nc - 2 - 2 * i)]
        for ch in chains:
            local_gates(ch)
        for ch in chains:
            local_matmuls(ch)
        m_f, n_f = state_step(chains[0], m_f, n_f, hf_ref)
        m_b, n_b = state_step(chains[1], m_b, n_b, hb_ref)
        m_f, n_f = state_step(chains[2], m_f, n_f, hf_ref)
        m_b, n_b = state_step(chains[3], m_b, n_b, hb_ref)
        return m_f, n_f, m_b, n_b

    assert nc % 2 == 0
    m0 = jnp.full((1, 1), NEG_BIG, F32)
    n0 = jnp.zeros((1, ML_HEAD_DIM), F32)
    lax.fori_loop(0, nc // 2, rec_step, (m0, n0, m0, n0))

    def out_step(c, carry):
        t0 = pl.multiple_of(c * L, L)
        hs = hf_ref[pl.ds(t0, L), :] + hb_ref[pl.ds(t0, L), :]
        mu = jnp.mean(hs, axis=-1, keepdims=True)
        hc = hs - mu
        var = jnp.mean(hc * hc, axis=-1, keepdims=True)
        hn = hc * lax.rsqrt(var + LN_EPS) * nw_ref[...]
        og = jax.nn.sigmoid(o_ref[pl.ds(t0, L), :].astype(F32))
        y_ref[pl.ds(t0, L), :] = (hn * og).astype(y_ref.dtype)
        return carry

    lax.fori_loop(0, nc, out_step, 0, unroll=4)


def _mlstm_branch(hp3, gates3, igate_b, fgate_b, norm_w):
    bsz, s, _ = hp3.shape
    assert s % ML_CHUNK == 0
    hd = ML_HEAD_DIM
    ng = 4 * ML_HEADS
    gates_t = gates3[..., :ng].transpose(0, 2, 1)
    gb = jnp.concatenate([igate_b, fgate_b]).astype(F32)
    gbl = jnp.pad(gb, (0, LANES - ng)).reshape(1, LANES)
    gbs = gb.reshape(ng, 1)
    nw = norm_w.astype(F32).reshape(1, ML_WIDTH)
    qb, kb, vb, ob = COL_ML_Q // hd, COL_ML_K // hd, COL_ML_V // hd, COL_ML_O // hd
    return pl.pallas_call(
        functools.partial(_mlstm_kernel, seq=s),
        grid=(bsz, ML_HEADS),
        in_specs=[
            pl.BlockSpec((None, s, hd), lambda b, h: (b, 0, qb + h)),
            pl.BlockSpec((None, s, hd), lambda b, h: (b, 0, kb + h)),
            pl.BlockSpec((None, s, hd), lambda b, h: (b, 0, vb + h)),
            pl.BlockSpec((None, s, hd), lambda b, h: (b, 0, ob + h)),
            pl.BlockSpec((None, s, LANES), lambda b, h: (b, 0, 0)),
            pl.BlockSpec((None, ng, s), lambda b, h: (b, 0, 0)),
            pl.BlockSpec((1, LANES), lambda b, h: (0, 0)),
            pl.BlockSpec((ng, 1), lambda b, h: (0, 0)),
            pl.BlockSpec((1, hd), lambda b, h: (0, h)),
        ],
        out_specs=pl.BlockSpec((None, s, hd), lambda b, h: (b, 0, h)),
        out_shape=jax.ShapeDtypeStruct((bsz, s, ML_WIDTH), BF16),
        scratch_shapes=[
            pltpu.VMEM((s, hd), F32),
            pltpu.VMEM((s, hd), F32),
            pltpu.VMEM((2, hd, hd), F32),
        ],
        compiler_params=pltpu.CompilerParams(
            dimension_semantics=("parallel", "parallel"), vmem_limit_bytes=VMEM_LIMIT),
        name="mlstm",
    )(hp3, hp3, hp3, hp3, gates3, gates_t, gbl, gbs, nw)


def _mix_kernel(x_ref, yna_ref, yml_ref, gna_ref, gml_ref, wna_ref, wml_ref, wo_ref, g_ref, b_ref,
                o_ref, *, alpha):
    a = _dot(yna_ref[...], wna_ref[...])
    mixed = jax.nn.sigmoid(gna_ref[...].astype(F32)) * a
    a = _dot(yml_ref[...], wml_ref[...])
    mixed = mixed + jax.nn.sigmoid(gml_ref[...].astype(F32)) * a
    z = alpha * x_ref[...] + _dot(mixed.astype(BF16), wo_ref[...])
    o_ref[...] = _layer_norm(z, g_ref[...], b_ref[...])


def _mix(x2, yna2, yml2, hp, wna, wml, wo, g, b, alpha):
    m, d = x2.shape
    tm = min(256, m)
    const = dict(pipeline_mode=pl.Buffered(1))
    return pl.pallas_call(
        functools.partial(_mix_kernel, alpha=alpha),
        grid=(m // tm,),
        in_specs=[
            pl.BlockSpec((tm, d), lambda i: (i, 0)),
            pl.BlockSpec((tm, NA_WIDTH), lambda i: (i, 0)),
            pl.BlockSpec((tm, ML_WIDTH), lambda i: (i, 0)),
            pl.BlockSpec((tm, d), lambda i: (i, COL_MERGE_NA // d)),
            pl.BlockSpec((tm, d), lambda i: (i, COL_MERGE_ML // d)),
            pl.BlockSpec((NA_WIDTH, d), lambda i: (0, 0), **const),
            pl.BlockSpec((ML_WIDTH, d), lambda i: (0, 0), **const),
            pl.BlockSpec((d, d), lambda i: (0, 0), **const),
            pl.BlockSpec((1, d), lambda i: (0, 0)),
            pl.BlockSpec((1, d), lambda i: (0, 0)),
        ],
        out_specs=pl.BlockSpec((tm, d), lambda i: (i, 0)),
        out_shape=jax.ShapeDtypeStruct((m, d), F32),
        compiler_params=pltpu.CompilerParams(
            dimension_semantics=("parallel",), vmem_limit_bytes=VMEM_LIMIT),
        name="mix",
    )(x2, yna2, yml2, hp, hp, wna, wml, wo, g, b)


def _gelu_exact(x):
    return 0.5 * x * (1.0 + lax.erf(x * np.float32(math.sqrt(0.5))))


def _ffn_kernel(x_ref, xp_ref, xn_ref, wg_ref, wv_ref, wd_ref, cw_ref, cb_ref, g_ref, b_ref,
                o_ref, xb_ref, acc_ref, *, alpha, seq, tm):
    i = pl.program_id(0)
    j = pl.program_id(1)
    hl = CONV_HALO

    @pl.when(j == 0)
    def _():
        _fill_halo_tile(xb_ref, x_ref, xp_ref, xn_ref, i, tm, seq)
        acc_ref[...] = jnp.zeros_like(acc_ref)

    ug = _dot(xb_ref[...], wg_ref[...])
    uv = _dot(xb_ref[hl:hl + tm, :], wv_ref[...])
    c = _conv3_rows(ug, None, cw_ref, cb_ref, slice(None), tm)
    h = (_gelu_exact(c) * uv).astype(BF16)
    acc_ref[...] += _dot(h, wd_ref[...])

    @pl.when(j == pl.num_programs(1) - 1)
    def _():
        z = alpha * x_ref[...] + acc_ref[...]
        o_ref[...] = _layer_norm(z, g_ref[...], b_ref[...])


def _ffn(x1, wg, wv, wd, cw, cb, g, b, alpha, seq):
    m, d = x1.shape
    dffp = wg.shape[1]
    tm = min(512, seq)
    tf = 512
    hl = CONV_HALO
    assert seq % tm == 0 and m % tm == 0 and dffp % tf == 0
    nb = tm // hl
    last = m // hl - 1
    return pl.pallas_call(
        functools.partial(_ffn_kernel, alpha=alpha, seq=seq, tm=tm),
        grid=(m // tm, dffp // tf),
        in_specs=[
            pl.BlockSpec((tm, d), lambda i, j: (i, 0)),
            pl.BlockSpec((hl, d), lambda i, j: (jnp.maximum(i * nb - 1, 0), 0)),
            pl.BlockSpec((hl, d), lambda i, j: (jnp.minimum((i + 1) * nb, last), 0)),
            pl.BlockSpec((d, tf), lambda i, j: (0, j)),
            pl.BlockSpec((d, tf), lambda i, j: (0, j)),
            pl.BlockSpec((tf, d), lambda i, j: (j, 0)),
            pl.BlockSpec((3, tf), lambda i, j: (0, j)),
            pl.BlockSpec((1, tf), lambda i, j: (0, j)),
            pl.BlockSpec((1, d), lambda i, j: (0, 0)),
            pl.BlockSpec((1, d), lambda i, j: (0, 0)),
        ],
        out_specs=pl.BlockSpec((tm, d), lambda i, j: (i, 0)),
        out_shape=jax.ShapeDtypeStruct((m, d), F32),
        scratch_shapes=[
            pltpu.VMEM((tm + 2 * hl, d), BF16),
            pltpu.VMEM((tm, d), F32),
        ],
        compiler_params=pltpu.CompilerParams(
            dimension_semantics=("parallel", "arbitrary"), vmem_limit_bytes=VMEM_LIMIT),
        name="ffn",
    )(x1, x1, x1, wg, wv, wd, cw, cb, g, b)


def _pad_cols(a, n):
    return jnp.pad(a, ((0, 0), (0, n - a.shape[1])))


def kernel(x, w_in, na_rpb, ml_conv_w, ml_conv_b, ml_igate_b, ml_fgate_b, ml_norm_w, w_branch_na,
           w_branch_ml, w_out, ln1_g, ln1_b, ffn_w_up, ffn_conv_w, ffn_conv_b, ffn_w_down, ln2_g, ln2_b):
    bsz, s, d = x.shape
    depth = w_in.shape[0]
    alpha = float((2 * depth) ** 0.25)
    dff = ffn_conv_w.shape[-1]
    dffp = -(-dff // 512) * 512
    m = bsz * s
    na3 = 3 * NA_WIDTH
    gates_at = na3 + 4 * ML_WIDTH
    merge_at = gates_at + 4 * ML_HEADS
    x2 = x.reshape(m, d)
    for l in range(depth):
        w = w_in[l]
        w_main = jnp.concatenate([w[:, merge_at:], w[:, :gates_at]], axis=1).astype(BF16)
        w_gate = _pad_cols(w[:, gates_at:merge_at], LANES).astype(BF16)
        hp, gates = _in_proj(x2, w_main, w_gate, ml_conv_w[l].astype(F32),
                             ml_conv_b[l].astype(F32).reshape(1, 2 * ML_WIDTH), s)
        hp3 = hp.reshape(bsz, s, HP_WIDTH)
        y_na = _na_attention(hp3, na_rpb[l])
        y_ml = _mlstm_branch(hp3, gates.reshape(bsz, s, LANES), ml_igate_b[l], ml_fgate_b[l],
                             ml_norm_w[l])
        x1 = _mix(x2, y_na.reshape(m, NA_WIDTH), y_ml.reshape(m, ML_WIDTH), hp,
                  w_branch_na[l].astype(BF16), w_branch_ml[l].astype(BF16), w_out[l].astype(BF16),
                  ln1_g[l].reshape(1, d), ln1_b[l].reshape(1, d), alpha)
        wg = _pad_cols(ffn_w_up[l][:, :dff], dffp).astype(BF16)
        wv = _pad_cols(ffn_w_up[l][:, dff:], dffp).astype(BF16)
        wd = jnp.pad(ffn_w_down[l], ((0, dffp - dff), (0, 0))).astype(BF16)
        cw = _pad_cols(ffn_conv_w[l].astype(F32), dffp)
        cb = _pad_cols(ffn_conv_b[l].astype(F32).reshape(1, dff), dffp)
        x2 = _ffn(x1, wg, wv, wd, cw, cb, ln2_g[l].reshape(1, d), ln2_b[l].reshape(1, d), alpha, s)
    return x2.reshape(bsz, s, d)
```

```python
import functools
import math

import numpy as np
import jax
import jax.numpy as jnp
from jax import lax
from jax.experimental import pallas as pl
from jax.experimental.pallas import tpu as pltpu

GRID_W = 64
NA_HEADS = 16
NA_HEAD_DIM = 64
NA_WIDTH = NA_HEADS * NA_HEAD_DIM
NA_WIN_ROWS = 8
NA_WIN_COLS = 16
NA_ROW_GROUP = 8
LOG2E = math.log2(math.e)
NA_Q_SCALE = NA_HEAD_DIM ** -0.5 * LOG2E
ML_HEADS = 4
ML_HEAD_DIM = 256
ML_WIDTH = ML_HEADS * ML_HEAD_DIM
ML_CHUNK = 128
LN_EPS = 1e-5
NEG_BIG = -1e30

LANES = 128
BF16_ROWS = 16
VMEM_LIMIT = 56 * 1024 * 1024

F32 = jnp.float32
BF16 = jnp.bfloat16

COL_MERGE_NA = 0
COL_MERGE_ML = 2048
COL_NA_Q = 4096
COL_NA_K = 5120
COL_NA_V = 6144
COL_ML_Q = 7168
COL_ML_K = 8192
COL_ML_V = 9216
COL_ML_O = 10240
HP_WIDTH = 11264


def _dot(a, b):
    return jnp.dot(a, b, preferred_element_type=F32)


def _dot_nt(a, b):
    return lax.dot_general(a, b, (((1,), (1,)), ((), ())), preferred_element_type=F32)


def _dot_tn(a, b):
    return lax.dot_general(a, b, (((0,), (0,)), ((), ())), preferred_element_type=F32)


def _layer_norm(z, g, b):
    mu = jnp.mean(z, axis=-1, keepdims=True)
    zc = z - mu
    var = jnp.mean(zc * zc, axis=-1, keepdims=True)
    return zc * lax.rsqrt(var + LN_EPS) * g + b


IN_TN = 1024
CONV_HALO = BF16_ROWS
CONV_COLS = 256


def _seq_edge_flags(i, tm, seq):
    has_prev = jnp.where((i * tm) % seq != 0, 1.0, 0.0).astype(F32)
    has_next = jnp.where(((i + 1) * tm) % seq != 0, 1.0, 0.0).astype(F32)
    return has_prev, has_next


def _fill_halo_tile(xb_ref, x_ref, xp_ref, xn_ref, i, tm, seq):
    hl = CONV_HALO
    has_prev, has_next = _seq_edge_flags(i, tm, seq)
    xb_ref[0:hl, :] = (xp_ref[...] * has_prev).astype(BF16)
    xb_ref[hl:hl + tm, :] = x_ref[...].astype(BF16)
    xb_ref[hl + tm:, :] = (xn_ref[...] * has_next).astype(BF16)


def _conv3_rows(u, u_ref, cw_ref, cb_ref, cols, tm):
    hl = CONV_HALO
    if u_ref is None:
        um1 = pltpu.roll(u, 1, 0)[hl:hl + tm]
        u0 = u[hl:hl + tm]
        up1 = pltpu.roll(u, u.shape[0] - 1, 0)[hl:hl + tm]
    else:
        u_ref[...] = u
        um1 = u_ref[hl - 1:hl - 1 + tm, :]
        u0 = u_ref[hl:hl + tm, :]
        up1 = u_ref[hl + 1:hl + 1 + tm, :]
    y = um1 * cw_ref[0:1, cols]
    y = y + u0 * cw_ref[1:2, cols]
    y = y + up1 * cw_ref[2:3, cols]
    return y + cb_ref[:, cols]


def _in_proj_kernel(x_ref, xp_ref, xn_ref, wm_ref, w_ref, wg_ref, cw_ref, cb_ref, o_ref, g_ref, xb_ref,
                    ub_ref, *, seq, tm, merge_blocks, conv_blocks):
    i = pl.program_id(0)
    j = pl.program_id(1)
    hl = CONV_HALO
    q_block, k_block = conv_blocks

    @pl.when(j == 0)
    def _():
        _fill_halo_tile(xb_ref, x_ref, xp_ref, xn_ref, i, tm, seq)
        g_ref[...] = _dot(xb_ref[hl:hl + tm, :], wg_ref[...])

    is_merge = j < merge_blocks
    is_conv = (j == q_block) | (j == k_block)

    @pl.when(is_merge)
    def _():
        o_ref[...] = _dot(xb_ref[hl:hl + tm, :], wm_ref[...]).astype(o_ref.dtype)

    @pl.when(jnp.logical_not(is_merge | is_conv))
    def _():
        scale = jnp.where(j == COL_NA_Q // IN_TN, NA_Q_SCALE, 1.0).astype(F32)
        o_ref[...] = (_dot(xb_ref[hl:hl + tm, :], w_ref[...]) * scale).astype(o_ref.dtype)

    @pl.when(is_conv)
    def _():
        post = jnp.where(j == k_block, ML_HEAD_DIM ** -0.5, 1.0).astype(F32)
        for n, c0 in enumerate(range(0, IN_TN, CONV_COLS)):
            cols = slice(c0, c0 + CONV_COLS)
            u = _dot(xb_ref[...], w_ref[:, cols])
            y = _conv3_rows(u, ub_ref.at[n % 2], cw_ref, cb_ref, cols, tm)
            y = y * jax.nn.sigmoid(y)
            o_ref[:, cols] = (y * post).astype(o_ref.dtype)


def _in_proj(x2, w_merge, w_rest, w_gate, conv_w, conv_b, seq):
    m, d = x2.shape
    n = w_merge.shape[1] + w_rest.shape[1]
    tm = min(1024, seq)
    tn = IN_TN
    hl = CONV_HALO
    assert seq % tm == 0 and m % tm == 0 and COL_ML_Q % tn == 0 and COL_ML_K == COL_ML_Q + tn
    assert n == HP_WIDTH and w_merge.shape[1] == COL_NA_Q and COL_NA_Q % tn == 0
    q_block = COL_ML_Q // tn
    merge_blocks = COL_NA_Q // tn
    rest_blocks = w_rest.shape[1] // tn
    nb = tm // hl
    last = m // hl - 1
    conv_idx = lambda i, j: (0, jnp.clip(j - q_block, 0, 1))
    return pl.pallas_call(
        functools.partial(_in_proj_kernel, seq=seq, tm=tm, merge_blocks=merge_blocks,
                          conv_blocks=(q_block, q_block + 1)),
        grid=(m // tm, n // tn),
        in_specs=[
            pl.BlockSpec((tm, d), lambda i, j: (i, 0)),
            pl.BlockSpec((hl, d), lambda i, j: (jnp.maximum(i * nb - 1, 0), 0)),
            pl.BlockSpec((hl, d), lambda i, j: (jnp.minimum((i + 1) * nb, last), 0)),
            pl.BlockSpec((d, tn), lambda i, j: (0, jnp.minimum(j, merge_blocks - 1))),
            pl.BlockSpec((d, tn), lambda i, j: (0, jnp.clip(j - merge_blocks, 0, rest_blocks - 1))),
            pl.BlockSpec((d, LANES), lambda i, j: (0, 0)),
            pl.BlockSpec((3, tn), conv_idx),
            pl.BlockSpec((1, tn), conv_idx),
        ],
        out_specs=[
            pl.BlockSpec((tm, tn), lambda i, j: (i, j)),
            pl.BlockSpec((tm, LANES), lambda i, j: (i, 0)),
        ],
        out_shape=[
            jax.ShapeDtypeStruct((m, n), BF16),
            jax.ShapeDtypeStruct((m, LANES), F32),
        ],
        scratch_shapes=[pltpu.VMEM((tm + 2 * hl, d), BF16),
                        pltpu.VMEM((2, tm + 2 * hl, CONV_COLS), F32)],
        compiler_params=pltpu.CompilerParams(
            dimension_semantics=("parallel", "arbitrary"), vmem_limit_bytes=VMEM_LIMIT),
        name="in_proj",
    )(x2, x2, x2, w_merge, w_rest, w_gate, conv_w, conv_b)


def _na_kernel(q_ref, k_ref, v_ref, bias_ref, o_ref, sca_ref, scb_ref, *, rows, group):
    kh = NA_WIN_ROWS
    nkeys = kh * GRID_W
    lane = lax.broadcasted_iota(jnp.int32, (2 * GRID_W, LANES), 1)
    qcol = lax.broadcasted_iota(jnp.int32, (2 * GRID_W, LANES), 0) & (GRID_W - 1)
    kcol = lane & (GRID_W - 1)
    wstart = jnp.clip(qcol - NA_WIN_COLS // 2, 0, GRID_W - NA_WIN_COLS)
    valid = (kcol >= wstart) & (kcol < wstart + NA_WIN_COLS)
    head_a = lax.broadcasted_iota(jnp.int32, (GRID_W, LANES), 1) < NA_HEAD_DIM

    def window(r):
        rs = jnp.clip(r - kh // 2, 0, rows - kh)
        return pl.multiple_of(rs * GRID_W, GRID_W), rs - r + (NA_WIN_ROWS - 1)

    def score_stage(g, sc_ref):
        for u in range(group):
            r = g * group + u
            kbase, _ = window(r)
            q = q_ref[pl.ds(pl.multiple_of(r * GRID_W, GRID_W), GRID_W), :]
            zero = jnp.zeros_like(q)
            qs = jnp.concatenate([jnp.where(head_a, q, zero), jnp.where(head_a, zero, q)], axis=0)
            sc_ref[u] = _dot_nt(qs, k_ref[pl.ds(kbase, nkeys), :])

    def value_stage(g, sc_ref):
        for u in range(group):
            r = g * group + u
            kbase, dr0 = window(r)
            s = []
            for c in range(kh // 2):
                bias = jnp.concatenate([bias_ref[0, dr0 + 2 * c], bias_ref[1, dr0 + 2 * c]], axis=0)
                sc_c = sc_ref[u, :, c * LANES:(c + 1) * LANES] + bias
                s.append(jnp.where(valid, sc_c, NEG_BIG))
            m = jnp.maximum(jnp.maximum(s[0], s[1]), jnp.maximum(s[2], s[3]))
            m = jnp.max(m, axis=-1, keepdims=True)
            p = [jnp.exp2(s_c - m) for s_c in s]
            l = jnp.sum((p[0] + p[1]) + (p[2] + p[3]), axis=-1, keepdims=True)
            pb = jnp.concatenate([p_c.astype(BF16) for p_c in p], axis=1)
            acc = _dot(pb, v_ref[pl.ds(kbase, nkeys), :]) / l
            o = jnp.where(head_a, acc[:GRID_W], acc[GRID_W:])
            o_ref[pl.ds(pl.multiple_of(r * GRID_W, GRID_W), GRID_W), :] = o.astype(o_ref.dtype)

    ngroups = rows // group
    score_stage(0, sca_ref)

    def pair_step(j, carry):
        score_stage(2 * j + 1, scb_ref)
        value_stage(2 * j, sca_ref)
        score_stage(2 * j + 2, sca_ref)
        value_stage(2 * j + 1, scb_ref)
        return carry

    lax.fori_loop(0, ngroups // 2 - 1, pair_step, 0)
    score_stage(ngroups - 1, scb_ref)
    value_stage(ngroups - 2, sca_ref)
    value_stage(ngroups - 1, scb_ref)


def _na_bias_expansion():
    nrel = 2 * NA_WIN_COLS - 1
    qc = np.arange(GRID_W)[:, None]
    kc = np.arange(GRID_W)[None, :]
    dc = np.clip(kc - qc + NA_WIN_COLS - 1, 0, nrel - 1)
    oh = np.zeros((2, nrel, GRID_W, 2, GRID_W), np.float32)
    for half in range(2):
        oh[half, dc, qc, half, kc] = 1.0
    return oh.reshape(2 * nrel, GRID_W * LANES)


def _na_bias_table(rpb):
    r2 = jnp.concatenate([rpb[:, :-1], rpb[:, 1:]], axis=-1)
    r2 = r2.reshape(NA_HEADS * (2 * NA_WIN_ROWS - 2), -1)
    t = jnp.dot(r2, jnp.asarray(_na_bias_expansion()), precision=lax.Precision.HIGHEST)
    return t.reshape(NA_HEADS // 2, 2, 2 * NA_WIN_ROWS - 2, GRID_W, LANES)


def _na_attention(hp3, rpb):
    bsz, s, _ = hp3.shape
    rows = s // GRID_W
    assert s % GRID_W == 0 and rows >= NA_WIN_ROWS
    table = _na_bias_table(rpb.astype(F32) * LOG2E)
    qb, kb, vb = COL_NA_Q // LANES, COL_NA_K // LANES, COL_NA_V // LANES
    group = min(NA_ROW_GROUP, rows // 2)
    assert rows % (2 * group) == 0
    sc_buf = pltpu.VMEM((group, 2 * GRID_W, NA_WIN_ROWS * GRID_W), F32)
    return pl.pallas_call(
        functools.partial(_na_kernel, rows=rows, group=group),
        grid=(bsz, NA_HEADS // 2),
        in_specs=[
            pl.BlockSpec((None, s, LANES), lambda b, h: (b, 0, qb + h)),
            pl.BlockSpec((None, s, LANES), lambda b, h: (b, 0, kb + h)),
            pl.BlockSpec((None, s, LANES), lambda b, h: (b, 0, vb + h)),
            pl.BlockSpec((None, 2, 2 * NA_WIN_ROWS - 2, GRID_W, LANES), lambda b, h: (h, 0, 0, 0, 0)),
        ],
        out_specs=pl.BlockSpec((None, s, LANES), lambda b, h: (b, 0, h)),
        out_shape=jax.ShapeDtypeStruct((bsz, s, NA_WIDTH), BF16),
        scratch_shapes=[sc_buf, sc_buf],
        compiler_params=pltpu.CompilerParams(
            dimension_semantics=("parallel", "parallel"), vmem_limit_bytes=VMEM_LIMIT),
        name="na_attn",
    )(hp3, hp3, hp3, table)


def _log_sigmoid(x):
    return jnp.minimum(x, 0.0) - jnp.log1p(jnp.exp(-jnp.abs(x)))


def _mlstm_kernel(q_ref, k_ref, v_ref, o_ref, g_ref, gt_ref, gbl_ref, gbs_ref, nw_ref, y_ref,
                  hf_ref, hb_ref, c_ref, *, seq):
    L = ML_CHUNK
    nc = seq // L
    head = pl.program_id(1)
    row = lax.broadcasted_iota(jnp.int32, (L, L), 0)
    col = lax.broadcasted_iota(jnp.int32, (L, L), 1)
    tril = row >= col
    triu = row <= col
    ng = 2 * ML_HEADS
    sub8 = lax.broadcasted_iota(jnp.int32, (ng, L), 0)

    c_ref[...] = jnp.zeros_like(c_ref)

    eye = row == col

    def local_scores(d, c):
        t0 = pl.multiple_of(c * L, L)
        q = q_ref[pl.ds(t0, L), :]
        k = k_ref[pl.ds(t0, L), :]
        v = v_ref[pl.ds(t0, L), :]
        return dict(d=d, t0=t0, q=q, k=k, v=v, s_qk=_dot_nt(q, k))

    def local_gates(ch):
        d, t0 = ch["d"], ch["t0"]
        valid = tril if d == 0 else triu
        ich = d * ML_HEADS + head
        pick = sub8 == ich
        gi = gt_ref[0:ng, pl.ds(t0, L)] + gbs_ref[0:ng, :]
        gf = _log_sigmoid(gt_ref[ng:2 * ng, pl.ds(t0, L)] + gbs_ref[ng:2 * ng, :])
        li_row = jnp.sum(jnp.where(pick, gi, 0.0), axis=0, keepdims=True)
        lf_row = jnp.sum(jnp.where(pick, gf, 0.0), axis=0, keepdims=True)
        g_tile = g_ref[pl.ds(t0, L), :] + gbl_ref[...]
        li_col = jnp.sum(jnp.where(col == ich, g_tile, 0.0), axis=1, keepdims=True)
        b_col = jnp.sum(jnp.where(valid, lf_row, 0.0), axis=1, keepdims=True)
        b_row = jnp.sum(jnp.where(eye, b_col, 0.0), axis=0, keepdims=True)
        b_last = jnp.sum(lf_row, axis=1, keepdims=True)
        dlog = jnp.where(valid, b_col - b_row + li_row, NEG_BIG)
        m_loc = jnp.max(dlog, axis=1, keepdims=True)
        a_loc = ch["s_qk"] * jnp.exp(dlog - m_loc)
        logw = b_last - b_col + li_col
        m_w = jnp.max(logw, axis=0, keepdims=True)
        kw = ch["k"].astype(F32) * jnp.exp(logw - m_w)
        ch.update(b_col=b_col, b_last=b_last, m_loc=m_loc, a_loc=a_loc,
                  rowsum=jnp.sum(a_loc, axis=1, keepdims=True), m_w=m_w, kw=kw,
                  ksum=jnp.sum(kw, axis=0, keepdims=True))

    def local_matmuls(ch):
        ch["h_loc"] = _dot(ch["a_loc"].astype(BF16), ch["v"])
        ch["kv"] = _dot_tn(ch["kw"].astype(BF16), ch["v"])

    def state_step(ch, m_st, n_st, h_ref):
        d = ch["d"]
        c_st = c_ref[d]
        q = ch["q"]
        qc = _dot(q, c_st.astype(BF16))
        qn = jnp.sum(q.astype(F32) * n_st, axis=1, keepdims=True)
        inter = ch["b_col"] + m_st
        m_t = jnp.maximum(ch["m_loc"], inter)
        w_inter = jnp.exp(inter - m_t)
        w_loc = jnp.exp(ch["m_loc"] - m_t)
        den = w_inter * qn + w_loc * ch["rowsum"]
        inv = 1.0 / jnp.maximum(jnp.abs(den), jnp.exp(-m_t))
        h_ref[pl.ds(ch["t0"], L), :] = (w_inter * inv) * qc + (w_loc * inv) * ch["h_loc"]
        m_new = jnp.maximum(ch["b_last"] + m_st, ch["m_w"])
        decay = jnp.exp(ch["b_last"] + m_st - m_new)
        scale = jnp.exp(ch["m_w"] - m_new)
        c_ref[d] = decay * c_st + scale * ch["kv"]
        return m_new, decay * n_st + scale * ch["ksum"]

    def rec_step(i, carry):
        m_f, n_f, m_b, n_b = carry
        chains = [local_scores(0, 2 * i), local_scores(1, nc - 1 - 2 * i),
                  local_scores(0, 2 * i + 1), local_scores(1, nc - 2 - 2 * i)]
        for ch in chains:
            local_gates(ch)
        for ch in chains:
            local_matmuls(ch)
        m_f, n_f = state_step(chains[0], m_f, n_f, hf_ref)
        m_b, n_b = state_step(chains[1], m_b, n_b, hb_ref)
        m_f, n_f = state_step(chains[2], m_f, n_f, hf_ref)
        m_b, n_b = state_step(chains[3], m_b, n_b, hb_ref)
        return m_f, n_f, m_b, n_b

    assert nc % 2 == 0
    m0 = jnp.full((1, 1), NEG_BIG, F32)
    n0 = jnp.zeros((1, ML_HEAD_DIM), F32)
    lax.fori_loop(0, nc // 2, rec_step, (m0, n0, m0, n0))

    def out_step(c, carry):
        t0 = pl.multiple_of(c * L, L)
        hs = hf_ref[pl.ds(t0, L), :] + hb_ref[pl.ds(t0, L), :]
        mu = jnp.mean(hs, axis=-1, keepdims=True)
        hc = hs - mu
        var = jnp.mean(hc * hc, axis=-1, keepdims=True)
        hn = hc * lax.rsqrt(var + LN_EPS) * nw_ref[...]
        og = jax.nn.sigmoid(o_ref[pl.ds(t0, L), :].astype(F32))
        y_ref[pl.ds(t0, L), :] = (hn * og).astype(y_ref.dtype)
        return carry

    lax.fori_loop(0, nc, out_step, 0, unroll=4)


def _mlstm_branch(hp3, gates3, igate_b, fgate_b, norm_w):
    bsz, s, _ = hp3.shape
    assert s % ML_CHUNK == 0
    hd = ML_HEAD_DIM
    ng = 4 * ML_HEADS
    gates_t = gates3[..., :ng].transpose(0, 2, 1)
    gb = jnp.concatenate([igate_b, fgate_b]).astype(F32)
    gbl = jnp.pad(gb, (0, LANES - ng)).reshape(1, LANES)
    gbs = gb.reshape(ng, 1)
    nw = norm_w.astype(F32).reshape(1, ML_WIDTH)
    qb, kb, vb, ob = COL_ML_Q // hd, COL_ML_K // hd, COL_ML_V // hd, COL_ML_O // hd
    return pl.pallas_call(
        functools.partial(_mlstm_kernel, seq=s),
        grid=(bsz, ML_HEADS),
        in_specs=[
            pl.BlockSpec((None, s, hd), lambda b, h: (b, 0, qb + h)),
            pl.BlockSpec((None, s, hd), lambda b, h: (b, 0, kb + h)),
            pl.BlockSpec((None, s, hd), lambda b, h: (b, 0, vb + h)),
            pl.BlockSpec((None, s, hd), lambda b, h: (b, 0, ob + h)),
            pl.BlockSpec((None, s, LANES), lambda b, h: (b, 0, 0)),
            pl.BlockSpec((None, ng, s), lambda b, h: (b, 0, 0)),
            pl.BlockSpec((1, LANES), lambda b, h: (0, 0)),
            pl.BlockSpec((ng, 1), lambda b, h: (0, 0)),
            pl.BlockSpec((1, hd), lambda b, h: (0, h)),
        ],
        out_specs=pl.BlockSpec((None, s, hd), lambda b, h: (b, 0, h)),
        out_shape=jax.ShapeDtypeStruct((bsz, s, ML_WIDTH), BF16),
        scratch_shapes=[
            pltpu.VMEM((s, hd), F32),
            pltpu.VMEM((s, hd), F32),
            pltpu.VMEM((2, hd, hd), F32),
        ],
        compiler_params=pltpu.CompilerParams(
            dimension_semantics=("parallel", "parallel"), vmem_limit_bytes=VMEM_LIMIT),
        name="mlstm",
    )(hp3, hp3, hp3, hp3, gates3, gates_t, gbl, gbs, nw)


MIX_SUB_ROWS = 256


def _mix_kernel(x_ref, yna_ref, yml_ref, gna_ref, gml_ref, wna_ref, wml_ref, wo_ref, g_ref, b_ref,
                o_ref, *, alpha):
    tm = x_ref.shape[0]
    for r0 in range(0, tm, MIX_SUB_ROWS):
        r = slice(r0, r0 + MIX_SUB_ROWS)
        a = _dot(yna_ref[r, :], wna_ref[...])
        mixed = jax.nn.sigmoid(gna_ref[r, :].astype(F32)) * a
        a = _dot(yml_ref[r, :], wml_ref[...])
        mixed = mixed + jax.nn.sigmoid(gml_ref[r, :].astype(F32)) * a
        z = alpha * x_ref[r, :] + _dot(mixed.astype(BF16), wo_ref[...])
        o_ref[r, :] = _layer_norm(z, g_ref[...], b_ref[...])


def _mix(x2, yna2, yml2, hp, wna, wml, wo, g, b, alpha):
    m, d = x2.shape
    tm = min(512, m)
    assert tm % MIX_SUB_ROWS == 0
    const = dict(pipeline_mode=pl.Buffered(1))
    return pl.pallas_call(
        functools.partial(_mix_kernel, alpha=alpha),
        grid=(m // tm,),
        in_specs=[
            pl.BlockSpec((tm, d), lambda i: (i, 0)),
            pl.BlockSpec((tm, NA_WIDTH), lambda i: (i, 0)),
            pl.BlockSpec((tm, ML_WIDTH), lambda i: (i, 0)),
            pl.BlockSpec((tm, d), lambda i: (i, COL_MERGE_NA // d)),
            pl.BlockSpec((tm, d), lambda i: (i, COL_MERGE_ML // d)),
            pl.BlockSpec((NA_WIDTH, d), lambda i: (0, 0), **const),
            pl.BlockSpec((ML_WIDTH, d), lambda i: (0, 0), **const),
            pl.BlockSpec((d, d), lambda i: (0, 0), **const),
            pl.BlockSpec((1, d), lambda i: (0, 0)),
            pl.BlockSpec((1, d), lambda i: (0, 0)),
        ],
        out_specs=pl.BlockSpec((tm, d), lambda i: (i, 0)),
        out_shape=jax.ShapeDtypeStruct((m, d), F32),
        compiler_params=pltpu.CompilerParams(
            dimension_semantics=("parallel",), vmem_limit_bytes=VMEM_LIMIT),
        name="mix",
    )(x2, yna2, yml2, hp, hp, wna, wml, wo, g, b)


def _gelu_exact(x):
    return 0.5 * x * (1.0 + lax.erf(x * np.float32(math.sqrt(0.5))))


def _ffn_kernel(x_ref, xp_ref, xn_ref, wg_ref, wv_ref, wd_ref, cw_ref, cb_ref, g_ref, b_ref,
                o_ref, xb_ref, acc_ref, *, alpha, seq, tm, tail):
    i = pl.program_id(0)
    j = pl.program_id(1)
    last = pl.num_programs(1) - 1
    hl = CONV_HALO

    @pl.when(j == 0)
    def _():
        _fill_halo_tile(xb_ref, x_ref, xp_ref, xn_ref, i, tm, seq)
        acc_ref[...] = jnp.zeros_like(acc_ref)

    def ff_block(width):
        cols = slice(0, width)
        ug = _dot(xb_ref[...], wg_ref[:, cols])
        uv = _dot(xb_ref[hl:hl + tm, :], wv_ref[:, cols])
        c = _conv3_rows(ug, None, cw_ref, cb_ref, cols, tm)
        h = (_gelu_exact(c) * uv).astype(BF16)
        acc_ref[...] += _dot(h, wd_ref[cols, :])

    @pl.when(j < last)
    def _():
        ff_block(wg_ref.shape[1])

    @pl.when(j == last)
    def _():
        ff_block(tail)
        z = alpha * x_ref[...] + acc_ref[...]
        o_ref[...] = _layer_norm(z, g_ref[...], b_ref[...])


def _ffn(x1, wg, wv, wd, cw, cb, g, b, alpha, seq):
    m, d = x1.shape
    dff = wg.shape[1]
    tm = min(512, seq)
    tf = 512
    hl = CONV_HALO
    assert seq % tm == 0 and m % tm == 0 and dff % LANES == 0
    nff = pl.cdiv(dff, tf)
    nb = tm // hl
    last = m // hl - 1
    return pl.pallas_call(
        functools.partial(_ffn_kernel, alpha=alpha, seq=seq, tm=tm, tail=dff - (nff - 1) * tf),
        grid=(m // tm, nff),
        in_specs=[
            pl.BlockSpec((tm, d), lambda i, j: (i, 0)),
            pl.BlockSpec((hl, d), lambda i, j: (jnp.maximum(i * nb - 1, 0), 0)),
            pl.BlockSpec((hl, d), lambda i, j: (jnp.minimum((i + 1) * nb, last), 0)),
            pl.BlockSpec((d, tf), lambda i, j: (0, j)),
            pl.BlockSpec((d, tf), lambda i, j: (0, j)),
            pl.BlockSpec((tf, d), lambda i, j: (j, 0)),
            pl.BlockSpec((3, tf), lambda i, j: (0, j)),
            pl.BlockSpec((1, tf), lambda i, j: (0, j)),
            pl.BlockSpec((1, d), lambda i, j: (0, 0)),
            pl.BlockSpec((1, d), lambda i, j: (0, 0)),
        ],
        out_specs=pl.BlockSpec((tm, d), lambda i, j: (i, 0)),
        out_shape=jax.ShapeDtypeStruct((m, d), F32),
        scratch_shapes=[
            pltpu.VMEM((tm + 2 * hl, d), BF16),
            pltpu.VMEM((tm, d), F32),
        ],
        compiler_params=pltpu.CompilerParams(
            dimension_semantics=("parallel", "arbitrary"), vmem_limit_bytes=VMEM_LIMIT),
        name="ffn",
    )(x1, x1, x1, wg, wv, wd, cw, cb, g, b)


def _pad_cols(a, n):
    return jnp.pad(a, ((0, 0), (0, n - a.shape[1])))


def kernel(x, w_in, na_rpb, ml_conv_w, ml_conv_b, ml_igate_b, ml_fgate_b, ml_norm_w, w_branch_na,
           w_branch_ml, w_out, ln1_g, ln1_b, ffn_w_up, ffn_conv_w, ffn_conv_b, ffn_w_down, ln2_g, ln2_b):
    bsz, s, d = x.shape
    depth = w_in.shape[0]
    alpha = float((2 * depth) ** 0.25)
    dff = ffn_conv_w.shape[-1]
    m = bsz * s
    na3 = 3 * NA_WIDTH
    gates_at = na3 + 4 * ML_WIDTH
    merge_at = gates_at + 4 * ML_HEADS
    x2 = x.reshape(m, d)
    for l in range(depth):
        w = w_in[l]
        w_gate = _pad_cols(w[:, gates_at:merge_at], LANES).astype(BF16)
        hp, gates = _in_proj(x2, w[:, merge_at:].astype(BF16), w[:, :gates_at].astype(BF16), w_gate,
                             ml_conv_w[l].astype(F32), ml_conv_b[l].astype(F32).reshape(1, 2 * ML_WIDTH), s)
        hp3 = hp.reshape(bsz, s, HP_WIDTH)
        y_na = _na_attention(hp3, na_rpb[l])
        y_ml = _mlstm_branch(hp3, gates.reshape(bsz, s, LANES), ml_igate_b[l], ml_fgate_b[l],
                             ml_norm_w[l])
        x1 = _mix(x2, y_na.reshape(m, NA_WIDTH), y_ml.reshape(m, ML_WIDTH), hp,
                  w_branch_na[l].astype(BF16), w_branch_ml[l].astype(BF16), w_out[l].astype(BF16),
                  ln1_g[l].reshape(1, d), ln1_b[l].reshape(1, d), alpha)
        x2 = _ffn(x1, ffn_w_up[l][:, :dff].astype(BF16), ffn_w_up[l][:, dff:].astype(BF16),
                  ffn_w_down[l].astype(BF16), ffn_conv_w[l].astype(F32),
                  ffn_conv_b[l].astype(F32).reshape(1, dff), ln2_g[l].reshape(1, d),
                  ln2_b[l].reshape(1, d), alpha, s)
    return x2.reshape(bsz, s, d)
```

```python
import functools
import math

import numpy as np
import jax
import jax.numpy as jnp
from jax import lax
from jax.experimental import pallas as pl
from jax.experimental.pallas import tpu as pltpu

GRID_W = 64
NA_HEADS = 16
NA_HEAD_DIM = 64
NA_WIDTH = NA_HEADS * NA_HEAD_DIM
NA_WIN_ROWS = 8
NA_WIN_COLS = 16
NA_ROW_GROUP = 8
LOG2E = math.log2(math.e)
NA_Q_SCALE = NA_HEAD_DIM ** -0.5 * LOG2E
ML_HEADS = 4
ML_HEAD_DIM = 256
ML_WIDTH = ML_HEADS * ML_HEAD_DIM
ML_CHUNK = 128
LN_EPS = 1e-5
NEG_BIG = -1e30

LANES = 128
BF16_ROWS = 16
VMEM_LIMIT = 56 * 1024 * 1024

F32 = jnp.float32
BF16 = jnp.bfloat16

COL_MERGE_NA = 0
COL_MERGE_ML = 2048
COL_NA_Q = 4096
COL_NA_K = 5120
COL_NA_V = 6144
COL_ML_Q = 7168
COL_ML_K = 8192
COL_ML_V = 9216
COL_ML_O = 10240
HP_WIDTH = 11264


def _dot(a, b):
    return jnp.dot(a, b, preferred_element_type=F32)


def _dot_nt(a, b):
    return lax.dot_general(a, b, (((1,), (1,)), ((), ())), preferred_element_type=F32)


def _dot_tn(a, b):
    return lax.dot_general(a, b, (((0,), (0,)), ((), ())), preferred_element_type=F32)


def _layer_norm(z, g, b):
    mu = jnp.mean(z, axis=-1, keepdims=True)
    zc = z - mu
    var = jnp.mean(zc * zc, axis=-1, keepdims=True)
    return zc * lax.rsqrt(var + LN_EPS) * g + b


IN_TN = 1024
CONV_HALO = BF16_ROWS
CONV_COLS = 256


def _seq_edge_flags(i, tm, seq):
    has_prev = jnp.where((i * tm) % seq != 0, 1.0, 0.0).astype(F32)
    has_next = jnp.where(((i + 1) * tm) % seq != 0, 1.0, 0.0).astype(F32)
    return has_prev, has_next


def _fill_halo_tile(xb_ref, x_ref, xp_ref, xn_ref, i, tm, seq):
    hl = CONV_HALO
    has_prev, has_next = _seq_edge_flags(i, tm, seq)
    xb_ref[0:hl, :] = (xp_ref[...] * has_prev).astype(BF16)
    xb_ref[hl:hl + tm, :] = x_ref[...].astype(BF16)
    xb_ref[hl + tm:, :] = (xn_ref[...] * has_next).astype(BF16)


def _conv3_rows(u, u_ref, cw_ref, cb_ref, cols, tm):
    hl = CONV_HALO
    if u_ref is None:
        um1 = pltpu.roll(u, 1, 0)[hl:hl + tm]
        u0 = u[hl:hl + tm]
        up1 = pltpu.roll(u, u.shape[0] - 1, 0)[hl:hl + tm]
    else:
        u_ref[...] = u
        um1 = u_ref[hl - 1:hl - 1 + tm, :]
        u0 = u_ref[hl:hl + tm, :]
        up1 = u_ref[hl + 1:hl + 1 + tm, :]
    y = um1 * cw_ref[0:1, cols]
    y = y + u0 * cw_ref[1:2, cols]
    y = y + up1 * cw_ref[2:3, cols]
    return y + cb_ref[:, cols]


def _in_proj_kernel(x_ref, xp_ref, xn_ref, wm_ref, w_ref, wg_ref, cw_ref, cb_ref, o_ref, g_ref, xb_ref,
                    ub_ref, *, seq, tm, merge_blocks, conv_blocks):
    i = pl.program_id(0)
    j = pl.program_id(1)
    hl = CONV_HALO
    q_block, k_block = conv_blocks

    @pl.when(j == 0)
    def _():
        _fill_halo_tile(xb_ref, x_ref, xp_ref, xn_ref, i, tm, seq)
        g_ref[...] = _dot(xb_ref[hl:hl + tm, :], wg_ref[...])

    is_merge = j < merge_blocks
    is_conv = (j == q_block) | (j == k_block)

    @pl.when(is_merge)
    def _():
        o_ref[...] = _dot(xb_ref[hl:hl + tm, :], wm_ref[...]).astype(o_ref.dtype)

    @pl.when(jnp.logical_not(is_merge | is_conv))
    def _():
        scale = jnp.where(j == COL_NA_Q // IN_TN, NA_Q_SCALE, 1.0).astype(F32)
        o_ref[...] = (_dot(xb_ref[hl:hl + tm, :], w_ref[...]) * scale).astype(o_ref.dtype)

    @pl.when(is_conv)
    def _():
        post = jnp.where(j == k_block, ML_HEAD_DIM ** -0.5, 1.0).astype(F32)
        for n, c0 in enumerate(range(0, IN_TN, CONV_COLS)):
            cols = slice(c0, c0 + CONV_COLS)
            u = _dot(xb_ref[...], w_ref[:, cols])
            y = _conv3_rows(u, ub_ref.at[n % 2], cw_ref, cb_ref, cols, tm)
            y = y * jax.nn.sigmoid(y)
            o_ref[:, cols] = (y * post).astype(o_ref.dtype)


def _in_proj(x2, w_merge, w_rest, w_gate, conv_w, conv_b, seq):
    m, d = x2.shape
    n = HP_WIDTH
    tm = min(1024, seq)
    tn = IN_TN
    hl = CONV_HALO
    assert seq % tm == 0 and m % tm == 0 and COL_ML_Q % tn == 0 and COL_ML_K == COL_ML_Q + tn
    assert w_merge.shape[1] == COL_NA_Q and COL_NA_Q % tn == 0 and w_rest.shape[1] >= n - COL_NA_Q
    q_block = COL_ML_Q // tn
    merge_blocks = COL_NA_Q // tn
    rest_blocks = (n - COL_NA_Q) // tn
    nb = tm // hl
    last = m // hl - 1
    conv_idx = lambda i, j: (0, jnp.clip(j - q_block, 0, 1))
    return pl.pallas_call(
        functools.partial(_in_proj_kernel, seq=seq, tm=tm, merge_blocks=merge_blocks,
                          conv_blocks=(q_block, q_block + 1)),
        grid=(m // tm, n // tn),
        in_specs=[
            pl.BlockSpec((tm, d), lambda i, j: (i, 0)),
            pl.BlockSpec((hl, d), lambda i, j: (jnp.maximum(i * nb - 1, 0), 0)),
            pl.BlockSpec((hl, d), lambda i, j: (jnp.minimum((i + 1) * nb, last), 0)),
            pl.BlockSpec((d, tn), lambda i, j: (0, jnp.minimum(j, merge_blocks - 1))),
            pl.BlockSpec((d, tn), lambda i, j: (0, jnp.where(j < merge_blocks, rest_blocks - 1,
                                                             j - merge_blocks))),
            pl.BlockSpec((d, LANES), lambda i, j: (0, 0)),
            pl.BlockSpec((3, tn), conv_idx),
            pl.BlockSpec((1, tn), conv_idx),
        ],
        out_specs=[
            pl.BlockSpec((tm, tn), lambda i, j: (i, j)),
            pl.BlockSpec((tm, LANES), lambda i, j: (i, 0)),
        ],
        out_shape=[
            jax.ShapeDtypeStruct((m, n), BF16),
            jax.ShapeDtypeStruct((m, LANES), F32),
        ],
        scratch_shapes=[pltpu.VMEM((tm + 2 * hl, d), BF16),
                        pltpu.VMEM((2, tm + 2 * hl, CONV_COLS), F32)],
        compiler_params=pltpu.CompilerParams(
            dimension_semantics=("parallel", "arbitrary"), vmem_limit_bytes=VMEM_LIMIT),
        name="in_proj",
    )(x2, x2, x2, w_merge, w_rest, w_gate, conv_w, conv_b)


def _na_kernel(q_ref, k_ref, v_ref, bias_ref, o_ref, sca_ref, scb_ref, *, rows, group):
    kh = NA_WIN_ROWS
    nkeys = kh * GRID_W
    lane = lax.broadcasted_iota(jnp.int32, (2 * GRID_W, LANES), 1)
    qcol = lax.broadcasted_iota(jnp.int32, (2 * GRID_W, LANES), 0) & (GRID_W - 1)
    kcol = lane & (GRID_W - 1)
    wstart = jnp.clip(qcol - NA_WIN_COLS // 2, 0, GRID_W - NA_WIN_COLS)
    valid = (kcol >= wstart) & (kcol < wstart + NA_WIN_COLS)
    head_a = lax.broadcasted_iota(jnp.int32, (GRID_W, LANES), 1) < NA_HEAD_DIM

    def window(r):
        rs = jnp.clip(r - kh // 2, 0, rows - kh)
        return pl.multiple_of(rs * GRID_W, GRID_W), rs - r + (NA_WIN_ROWS - 1)

    def score_stage(g, sc_ref):
        for u in range(group):
            r = g * group + u
            kbase, _ = window(r)
            q = q_ref[pl.ds(pl.multiple_of(r * GRID_W, GRID_W), GRID_W), :]
            zero = jnp.zeros_like(q)
            qs = jnp.concatenate([jnp.where(head_a, q, zero), jnp.where(head_a, zero, q)], axis=0)
            sc_ref[u] = _dot_nt(qs, k_ref[pl.ds(kbase, nkeys), :])

    def value_stage(g, sc_ref):
        for u in range(group):
            r = g * group + u
            kbase, dr0 = window(r)
            s = []
            for c in range(kh // 2):
                bias = jnp.concatenate([bias_ref[0, dr0 + 2 * c], bias_ref[1, dr0 + 2 * c]], axis=0)
                sc_c = sc_ref[u, :, c * LANES:(c + 1) * LANES] + bias
                s.append(jnp.where(valid, sc_c, NEG_BIG))
            m = jnp.maximum(jnp.maximum(s[0], s[1]), jnp.maximum(s[2], s[3]))
            m = jnp.max(m, axis=-1, keepdims=True)
            p = [jnp.exp2(s_c - m) for s_c in s]
            l = jnp.sum((p[0] + p[1]) + (p[2] + p[3]), axis=-1, keepdims=True)
            pb = jnp.concatenate([p_c.astype(BF16) for p_c in p], axis=1)
            acc = _dot(pb, v_ref[pl.ds(kbase, nkeys), :]) / l
            o = jnp.where(head_a, acc[:GRID_W], acc[GRID_W:])
            o_ref[pl.ds(pl.multiple_of(r * GRID_W, GRID_W), GRID_W), :] = o.astype(o_ref.dtype)

    ngroups = rows // group
    score_stage(0, sca_ref)

    def pair_step(j, carry):
        score_stage(2 * j + 1, scb_ref)
        value_stage(2 * j, sca_ref)
        score_stage(2 * j + 2, sca_ref)
        value_stage(2 * j + 1, scb_ref)
        return carry

    lax.fori_loop(0, ngroups // 2 - 1, pair_step, 0)
    score_stage(ngroups - 1, scb_ref)
    value_stage(ngroups - 2, sca_ref)
    value_stage(ngroups - 1, scb_ref)


def _na_bias_expansion():
    nrel = 2 * NA_WIN_COLS - 1
    qc = np.arange(GRID_W)[:, None]
    kc = np.arange(GRID_W)[None, :]
    dc = np.clip(kc - qc + NA_WIN_COLS - 1, 0, nrel - 1)
    oh = np.zeros((2, nrel, GRID_W, 2, GRID_W), np.float32)
    for half in range(2):
        oh[half, dc, qc, half, kc] = 1.0
    return oh.reshape(2 * nrel, GRID_W * LANES)


def _na_bias_table(rpb):
    r2 = jnp.concatenate([rpb[:, :-1], rpb[:, 1:]], axis=-1)
    r2 = r2.reshape(NA_HEADS * (2 * NA_WIN_ROWS - 2), -1)
    t = jnp.dot(r2, jnp.asarray(_na_bias_expansion()), precision=lax.Precision.HIGHEST)
    return t.reshape(NA_HEADS // 2, 2, 2 * NA_WIN_ROWS - 2, GRID_W, LANES)


def _na_attention(hp3, rpb):
    bsz, s, _ = hp3.shape
    rows = s // GRID_W
    assert s % GRID_W == 0 and rows >= NA_WIN_ROWS
    table = _na_bias_table(rpb.astype(F32) * LOG2E)
    qb, kb, vb = COL_NA_Q // LANES, COL_NA_K // LANES, COL_NA_V // LANES
    group = min(NA_ROW_GROUP, rows // 2)
    assert rows % (2 * group) == 0
    sc_buf = pltpu.VMEM((group, 2 * GRID_W, NA_WIN_ROWS * GRID_W), F32)
    return pl.pallas_call(
        functools.partial(_na_kernel, rows=rows, group=group),
        grid=(bsz, NA_HEADS // 2),
        in_specs=[
            pl.BlockSpec((None, s, LANES), lambda b, h: (b, 0, qb + h)),
            pl.BlockSpec((None, s, LANES), lambda b, h: (b, 0, kb + h)),
            pl.BlockSpec((None, s, LANES), lambda b, h: (b, 0, vb + h)),
            pl.BlockSpec((None, 2, 2 * NA_WIN_ROWS - 2, GRID_W, LANES), lambda b, h: (h, 0, 0, 0, 0)),
        ],
        out_specs=pl.BlockSpec((None, s, LANES), lambda b, h: (b, 0, h)),
        out_shape=jax.ShapeDtypeStruct((bsz, s, NA_WIDTH), BF16),
        scratch_shapes=[sc_buf, sc_buf],
        compiler_params=pltpu.CompilerParams(
            dimension_semantics=("parallel", "parallel"), vmem_limit_bytes=VMEM_LIMIT),
        name="na_attn",
    )(hp3, hp3, hp3, table)


def _log_sigmoid(x):
    return jnp.minimum(x, 0.0) - jnp.log1p(jnp.exp(-jnp.abs(x)))


def _mlstm_kernel(q_ref, k_ref, v_ref, o_ref, g_ref, gt_ref, gbl_ref, gbs_ref, nw_ref, y_ref,
                  hf_ref, hb_ref, c_ref, *, seq):
    L = ML_CHUNK
    nc = seq // L
    head = pl.program_id(1)
    row = lax.broadcasted_iota(jnp.int32, (L, L), 0)
    col = lax.broadcasted_iota(jnp.int32, (L, L), 1)
    tril = row >= col
    triu = row <= col
    ng = 2 * ML_HEADS
    sub8 = lax.broadcasted_iota(jnp.int32, (ng, L), 0)

    c_ref[...] = jnp.zeros_like(c_ref)

    eye = row == col

    def local_scores(d, c):
        t0 = pl.multiple_of(c * L, L)
        q = q_ref[pl.ds(t0, L), :]
        k = k_ref[pl.ds(t0, L), :]
        v = v_ref[pl.ds(t0, L), :]
        return dict(d=d, t0=t0, q=q, k=k, v=v, s_qk=_dot_nt(q, k))

    def local_gates(ch):
        d, t0 = ch["d"], ch["t0"]
        valid = tril if d == 0 else triu
        ich = d * ML_HEADS + head
        pick = sub8 == ich
        gi = gt_ref[0:ng, pl.ds(t0, L)] + gbs_ref[0:ng, :]
        gf = _log_sigmoid(gt_ref[ng:2 * ng, pl.ds(t0, L)] + gbs_ref[ng:2 * ng, :])
        li_row = jnp.sum(jnp.where(pick, gi, 0.0), axis=0, keepdims=True)
        lf_row = jnp.sum(jnp.where(pick, gf, 0.0), axis=0, keepdims=True)
        g_tile = g_ref[pl.ds(t0, L), :] + gbl_ref[...]
        li_col = jnp.sum(jnp.where(col == ich, g_tile, 0.0), axis=1, keepdims=True)
        b_col = jnp.sum(jnp.where(valid, lf_row, 0.0), axis=1, keepdims=True)
        b_row = jnp.sum(jnp.where(eye, b_col, 0.0), axis=0, keepdims=True)
        b_last = jnp.sum(lf_row, axis=1, keepdims=True)
        dlog = jnp.where(valid, b_col - b_row + li_row, NEG_BIG)
        m_loc = jnp.max(dlog, axis=1, keepdims=True)
        a_loc = ch["s_qk"] * jnp.exp(dlog - m_loc)
        logw = b_last - b_col + li_col
        ch.update(b_col=b_col, b_last=b_last, m_loc=m_loc, a_loc=a_loc,
                  rowsum=jnp.sum(a_loc, axis=1, keepdims=True), logw=logw,
                  m_w=jnp.max(logw, axis=0, keepdims=True))

    def state_weights(ch, m_st):
        m_new = jnp.maximum(ch["b_last"] + m_st, ch["m_w"])
        kw = ch["k"].astype(F32) * jnp.exp(ch["logw"] - m_new)
        ch.update(m_in=m_st, decay=jnp.exp(ch["b_last"] + m_st - m_new), kw=kw,
                  ksum=jnp.sum(kw, axis=0, keepdims=True))
        return m_new

    def local_matmuls(ch):
        ch["h_loc"] = _dot(ch["a_loc"].astype(BF16), ch["v"])
        ch["kv"] = _dot_tn(ch["kw"].astype(BF16), ch["v"])

    def state_step(ch, n_st, h_ref):
        d = ch["d"]
        c_st = c_ref[d]
        q = ch["q"]
        qc = _dot(q, c_st.astype(BF16))
        qn = jnp.sum(q.astype(F32) * n_st, axis=1, keepdims=True)
        inter = ch["b_col"] + ch["m_in"]
        m_t = jnp.maximum(ch["m_loc"], inter)
        w_inter = jnp.exp(inter - m_t)
        w_loc = jnp.exp(ch["m_loc"] - m_t)
        den = w_inter * qn + w_loc * ch["rowsum"]
        inv = 1.0 / jnp.maximum(jnp.abs(den), jnp.exp(-m_t))
        h_ref[pl.ds(ch["t0"], L), :] = (w_inter * inv) * qc + (w_loc * inv) * ch["h_loc"]
        c_ref[d] = ch["decay"] * c_st + ch["kv"]
        return ch["decay"] * n_st + ch["ksum"]

    def rec_step(i, carry):
        m_f, n_f, m_b, n_b = carry
        chains = [local_scores(0, 2 * i), local_scores(1, nc - 1 - 2 * i),
                  local_scores(0, 2 * i + 1), local_scores(1, nc - 2 - 2 * i)]
        for ch in chains:
            local_gates(ch)
        m_f = state_weights(chains[0], m_f)
        m_b = state_weights(chains[1], m_b)
        m_f = state_weights(chains[2], m_f)
        m_b = state_weights(chains[3], m_b)
        for ch in chains:
            local_matmuls(ch)
        n_f = state_step(chains[0], n_f, hf_ref)
        n_b = state_step(chains[1], n_b, hb_ref)
        n_f = state_step(chains[2], n_f, hf_ref)
        n_b = state_step(chains[3], n_b, hb_ref)
        return m_f, n_f, m_b, n_b

    assert nc % 2 == 0
    m0 = jnp.full((1, 1), NEG_BIG, F32)
    n0 = jnp.zeros((1, ML_HEAD_DIM), F32)
    lax.fori_loop(0, nc // 2, rec_step, (m0, n0, m0, n0))

    def out_step(c, carry):
        t0 = pl.multiple_of(c * L, L)
        hs = hf_ref[pl.ds(t0, L), :] + hb_ref[pl.ds(t0, L), :]
        mu = jnp.mean(hs, axis=-1, keepdims=True)
        hc = hs - mu
        var = jnp.mean(hc * hc, axis=-1, keepdims=True)
        hn = hc * lax.rsqrt(var + LN_EPS) * nw_ref[...]
        og = jax.nn.sigmoid(o_ref[pl.ds(t0, L), :].astype(F32))
        y_ref[pl.ds(t0, L), :] = (hn * og).astype(y_ref.dtype)
        return carry

    lax.fori_loop(0, nc, out_step, 0, unroll=4)


def _mlstm_branch(hp3, gates3, igate_b, fgate_b, norm_w):
    bsz, s, _ = hp3.shape
    assert s % ML_CHUNK == 0
    hd = ML_HEAD_DIM
    ng = 4 * ML_HEADS
    gates_t = gates3[..., :ng].transpose(0, 2, 1)
    gb = jnp.concatenate([igate_b, fgate_b]).astype(F32)
    gbl = jnp.pad(gb, (0, LANES - ng)).reshape(1, LANES)
    gbs = gb.reshape(ng, 1)
    nw = norm_w.astype(F32).reshape(1, ML_WIDTH)
    qb, kb, vb, ob = COL_ML_Q // hd, COL_ML_K // hd, COL_ML_V // hd, COL_ML_O // hd
    return pl.pallas_call(
        functools.partial(_mlstm_kernel, seq=s),
        grid=(bsz, ML_HEADS),
        in_specs=[
            pl.BlockSpec((None, s, hd), lambda b, h: (b, 0, qb + h)),
            pl.BlockSpec((None, s, hd), lambda b, h: (b, 0, kb + h)),
            pl.BlockSpec((None, s, hd), lambda b, h: (b, 0, vb + h)),
            pl.BlockSpec((None, s, hd), lambda b, h: (b, 0, ob + h)),
            pl.BlockSpec((None, s, LANES), lambda b, h: (b, 0, 0)),
            pl.BlockSpec((None, ng, s), lambda b, h: (b, 0, 0)),
            pl.BlockSpec((1, LANES), lambda b, h: (0, 0)),
            pl.BlockSpec((ng, 1), lambda b, h: (0, 0)),
            pl.BlockSpec((1, hd), lambda b, h: (0, h)),
        ],
        out_specs=pl.BlockSpec((None, s, hd), lambda b, h: (b, 0, h)),
        out_shape=jax.ShapeDtypeStruct((bsz, s, ML_WIDTH), BF16),
        scratch_shapes=[
            pltpu.VMEM((s, hd), F32),
            pltpu.VMEM((s, hd), F32),
            pltpu.VMEM((2, hd, hd), F32),
        ],
        compiler_params=pltpu.CompilerParams(
            dimension_semantics=("parallel", "parallel"), vmem_limit_bytes=VMEM_LIMIT),
        name="mlstm",
    )(hp3, hp3, hp3, hp3, gates3, gates_t, gbl, gbs, nw)


MIX_SUB_ROWS = 256


def _mix_kernel(x_ref, yna_ref, yml_ref, gna_ref, gml_ref, wna_ref, wml_ref, wo_ref, g_ref, b_ref,
                o_ref, *, alpha):
    tm = x_ref.shape[0]
    for r0 in range(0, tm, MIX_SUB_ROWS):
        r = slice(r0, r0 + MIX_SUB_ROWS)
        a = _dot(yna_ref[r, :], wna_ref[...])
        mixed = jax.nn.sigmoid(gna_ref[r, :].astype(F32)) * a
        a = _dot(yml_ref[r, :], wml_ref[...])
        mixed = mixed + jax.nn.sigmoid(gml_ref[r, :].astype(F32)) * a
        z = alpha * x_ref[r, :] + _dot(mixed.astype(BF16), wo_ref[...])
        o_ref[r, :] = _layer_norm(z, g_ref[...], b_ref[...])


def _mix(x2, yna2, yml2, hp, wna, wml, wo, g, b, alpha):
    m, d = x2.shape
    tm = min(512, m)
    assert tm % MIX_SUB_ROWS == 0
    const = dict(pipeline_mode=pl.Buffered(1))
    return pl.pallas_call(
        functools.partial(_mix_kernel, alpha=alpha),
        grid=(m // tm,),
        in_specs=[
            pl.BlockSpec((tm, d), lambda i: (i, 0)),
            pl.BlockSpec((tm, NA_WIDTH), lambda i: (i, 0)),
            pl.BlockSpec((tm, ML_WIDTH), lambda i: (i, 0)),
            pl.BlockSpec((tm, d), lambda i: (i, COL_MERGE_NA // d)),
            pl.BlockSpec((tm, d), lambda i: (i, COL_MERGE_ML // d)),
            pl.BlockSpec((NA_WIDTH, d), lambda i: (0, 0), **const),
            pl.BlockSpec((ML_WIDTH, d), lambda i: (0, 0), **const),
            pl.BlockSpec((d, d), lambda i: (0, 0), **const),
            pl.BlockSpec((1, d), lambda i: (0, 0)),
            pl.BlockSpec((1, d), lambda i: (0, 0)),
        ],
        out_specs=pl.BlockSpec((tm, d), lambda i: (i, 0)),
        out_shape=jax.ShapeDtypeStruct((m, d), F32),
        compiler_params=pltpu.CompilerParams(
            dimension_semantics=("parallel",), vmem_limit_bytes=VMEM_LIMIT),
        name="mix",
    )(x2, yna2, yml2, hp, hp, wna, wml, wo, g, b)


FFN_TF = 512


def _gelu_exact(x):
    return 0.5 * x * (1.0 + lax.erf(x * np.float32(math.sqrt(0.5))))


def _ffn_kernel(x_ref, xp_ref, xn_ref, wg_ref, wv_ref, wd_ref, cw_ref, cb_ref, g_ref, b_ref,
                o_ref, xb_ref, acc_ref, *, alpha, seq, tm, tail):
    i = pl.program_id(0)
    j = pl.program_id(1)
    last = pl.num_programs(1) - 1
    hl = CONV_HALO

    @pl.when(j == 0)
    def _():
        _fill_halo_tile(xb_ref, x_ref, xp_ref, xn_ref, i, tm, seq)
        acc_ref[...] = jnp.zeros_like(acc_ref)

    def ff_block(width):
        cols = slice(0, width)
        ug = _dot(xb_ref[...], wg_ref[:, cols])
        uv = _dot(xb_ref[hl:hl + tm, :], wv_ref[:, cols])
        c = _conv3_rows(ug, None, cw_ref, cb_ref, cols, tm)
        h = (_gelu_exact(c) * uv).astype(BF16)
        acc_ref[...] += _dot(h, wd_ref[cols, :])

    @pl.when(j < last)
    def _():
        ff_block(wg_ref.shape[1])

    @pl.when(j == last)
    def _():
        ff_block(tail)
        z = alpha * x_ref[...] + acc_ref[...]
        o_ref[...] = _layer_norm(z, g_ref[...], b_ref[...])


def _ffn(x1, wg, wv, wd, cw, cb, g, b, alpha, seq, dff):
    m, d = x1.shape
    tm = min(512, seq)
    tf = FFN_TF
    hl = CONV_HALO
    assert seq % tm == 0 and m % tm == 0 and dff % LANES == 0 and wg.shape[1] % tf == 0
    nff = pl.cdiv(dff, tf)
    nb = tm // hl
    last = m // hl - 1
    return pl.pallas_call(
        functools.partial(_ffn_kernel, alpha=alpha, seq=seq, tm=tm, tail=dff - (nff - 1) * tf),
        grid=(m // tm, nff),
        in_specs=[
            pl.BlockSpec((tm, d), lambda i, j: (i, 0)),
            pl.BlockSpec((hl, d), lambda i, j: (jnp.maximum(i * nb - 1, 0), 0)),
            pl.BlockSpec((hl, d), lambda i, j: (jnp.minimum((i + 1) * nb, last), 0)),
            pl.BlockSpec((d, tf), lambda i, j: (0, j)),
            pl.BlockSpec((d, tf), lambda i, j: (0, j)),
            pl.BlockSpec((tf, d), lambda i, j: (j, 0)),
            pl.BlockSpec((3, tf), lambda i, j: (0, j)),
            pl.BlockSpec((1, tf), lambda i, j: (0, j)),
            pl.BlockSpec((1, d), lambda i, j: (0, 0)),
            pl.BlockSpec((1, d), lambda i, j: (0, 0)),
        ],
        out_specs=pl.BlockSpec((tm, d), lambda i, j: (i, 0)),
        out_shape=jax.ShapeDtypeStruct((m, d), F32),
        scratch_shapes=[
            pltpu.VMEM((tm + 2 * hl, d), BF16),
            pltpu.VMEM((tm, d), F32),
        ],
        compiler_params=pltpu.CompilerParams(
            dimension_semantics=("parallel", "arbitrary"), vmem_limit_bytes=VMEM_LIMIT),
        name="ffn",
    )(x1, x1, x1, wg, wv, wd, cw, cb, g, b)


def _pad_cols(a, n):
    return jnp.pad(a, ((0, 0), (0, n - a.shape[1])))


def kernel(x, w_in, na_rpb, ml_conv_w, ml_conv_b, ml_igate_b, ml_fgate_b, ml_norm_w, w_branch_na,
           w_branch_ml, w_out, ln1_g, ln1_b, ffn_w_up, ffn_conv_w, ffn_conv_b, ffn_w_down, ln2_g, ln2_b):
    bsz, s, d = x.shape
    depth = w_in.shape[0]
    alpha = float((2 * depth) ** 0.25)
    dff = ffn_conv_w.shape[-1]
    m = bsz * s
    na3 = 3 * NA_WIDTH
    gates_at = na3 + 4 * ML_WIDTH
    merge_at = gates_at + 4 * ML_HEADS
    x2 = x.reshape(m, d)
    for l in range(depth):
        w = w_in[l]
        w_gate = _pad_cols(w[:, gates_at:merge_at], LANES).astype(BF16)
        wb = w.astype(BF16)
        hp, gates = _in_proj(x2, wb[:, merge_at:], wb, w_gate,
                             ml_conv_w[l].astype(F32), ml_conv_b[l].astype(F32).reshape(1, 2 * ML_WIDTH), s)
        hp3 = hp.reshape(bsz, s, HP_WIDTH)
        y_na = _na_attention(hp3, na_rpb[l])
        y_ml = _mlstm_branch(hp3, gates.reshape(bsz, s, LANES), ml_igate_b[l], ml_fgate_b[l],
                             ml_norm_w[l])
        x1 = _mix(x2, y_na.reshape(m, NA_WIDTH), y_ml.reshape(m, ML_WIDTH), hp,
                  w_branch_na[l].astype(BF16), w_branch_ml[l].astype(BF16), w_out[l].astype(BF16),
                  ln1_g[l].reshape(1, d), ln1_b[l].reshape(1, d), alpha)
        dffp = pl.cdiv(dff, FFN_TF) * FFN_TF
        wg = _pad_cols(ffn_w_up[l][:, :dff], dffp).astype(BF16)
        wv = _pad_cols(ffn_w_up[l][:, dff:], dffp).astype(BF16)
        wd = jnp.pad(ffn_w_down[l], ((0, dffp - dff), (0, 0))).astype(BF16)
        cw = _pad_cols(ffn_conv_w[l].astype(F32), dffp)
        cb = _pad_cols(ffn_conv_b[l].astype(F32).reshape(1, dff), dffp)
        x2 = _ffn(x1, wg, wv, wd, cw, cb, ln2_g[l].reshape(1, d), ln2_b[l].reshape(1, d), alpha, s, dff)
    return x2.reshape(bsz, s, d)
```

```python
import functools
import math

import numpy as np
import jax
import jax.numpy as jnp
from jax import lax
from jax.experimental import pallas as pl
from jax.experimental.pallas import tpu as pltpu

GRID_W = 64
NA_HEADS = 16
NA_HEAD_DIM = 64
NA_WIDTH = NA_HEADS * NA_HEAD_DIM
NA_WIN_ROWS = 8
NA_WIN_COLS = 16
NA_ROW_GROUP = 8
LOG2E = math.log2(math.e)
NA_Q_SCALE = NA_HEAD_DIM ** -0.5 * LOG2E
ML_HEADS = 4
ML_HEAD_DIM = 256
ML_WIDTH = ML_HEADS * ML_HEAD_DIM
ML_CHUNK = 128
LN_EPS = 1e-5
NEG_BIG = -1e30

LANES = 128
BF16_ROWS = 16
VMEM_LIMIT = 56 * 1024 * 1024
VMEM_LIMIT_HIGH = 60 * 1024 * 1024

F32 = jnp.float32
BF16 = jnp.bfloat16

COL_MERGE_NA = 0
COL_MERGE_ML = 2048
COL_NA_Q = 4096
COL_NA_K = 5120
COL_NA_V = 6144
COL_ML_V = 7168
COL_ML_O = 8192
COL_ML_Q = 9216
COL_ML_K = 10240
HP_WIDTH = 11264


def _dot(a, b):
    return jnp.dot(a, b, preferred_element_type=F32)


def _dot_nt(a, b):
    return lax.dot_general(a, b, (((1,), (1,)), ((), ())), preferred_element_type=F32)


def _dot_tn(a, b):
    return lax.dot_general(a, b, (((0,), (0,)), ((), ())), preferred_element_type=F32)


def _layer_norm(z, g, b):
    mu = jnp.mean(z, axis=-1, keepdims=True)
    zc = z - mu
    var = jnp.mean(zc * zc, axis=-1, keepdims=True)
    return zc * lax.rsqrt(var + LN_EPS) * g + b


IN_TN = 1024
CONV_HALO = BF16_ROWS
CONV_COLS = 256


def _seq_edge_flags(i, tm, seq):
    has_prev = jnp.where((i * tm) % seq != 0, 1.0, 0.0).astype(F32)
    has_next = jnp.where(((i + 1) * tm) % seq != 0, 1.0, 0.0).astype(F32)
    return has_prev, has_next


def _fill_halo_tile(xb_ref, x_ref, xp_ref, xn_ref, i, tm, seq):
    hl = CONV_HALO
    has_prev, has_next = _seq_edge_flags(i, tm, seq)
    xb_ref[0:hl, :] = (xp_ref[...] * has_prev).astype(BF16)
    xb_ref[hl:hl + tm, :] = x_ref[...].astype(BF16)
    xb_ref[hl + tm:, :] = (xn_ref[...] * has_next).astype(BF16)


def _conv3_rows(u, u_ref, cw_ref, cb_ref, cols, tm):
    hl = CONV_HALO
    if u_ref is None:
        um1 = pltpu.roll(u, 1, 0)[hl:hl + tm]
        u0 = u[hl:hl + tm]
        up1 = pltpu.roll(u, u.shape[0] - 1, 0)[hl:hl + tm]
    else:
        u_ref[...] = u
        um1 = u_ref[hl - 1:hl - 1 + tm, :]
        u0 = u_ref[hl:hl + tm, :]
        up1 = u_ref[hl + 1:hl + 1 + tm, :]
    y = um1 * cw_ref[0:1, cols]
    y = y + u0 * cw_ref[1:2, cols]
    y = y + up1 * cw_ref[2:3, cols]
    return y + cb_ref[:, cols]


def _in_proj_kernel(x_ref, xp_ref, xn_ref, wm_ref, w_ref, wg_ref, cw_ref, cb_ref, o_ref, g_ref, xb_ref,
                    ub_ref, *, seq, tm, merge_blocks, conv_blocks):
    i = pl.program_id(0)
    j = pl.program_id(1)
    hl = CONV_HALO
    q_block, k_block = conv_blocks

    @pl.when(j == 0)
    def _():
        _fill_halo_tile(xb_ref, x_ref, xp_ref, xn_ref, i, tm, seq)
        g_ref[...] = _dot(xb_ref[hl:hl + tm, :], wg_ref[...])

    is_merge = j < merge_blocks
    is_conv = (j == q_block) | (j == k_block)

    @pl.when(is_merge)
    def _():
        o_ref[...] = _dot(xb_ref[hl:hl + tm, :], wm_ref[...]).astype(o_ref.dtype)

    @pl.when(jnp.logical_not(is_merge | is_conv))
    def _():
        scale = jnp.where(j == COL_NA_Q // IN_TN, NA_Q_SCALE, 1.0).astype(F32)
        o_ref[...] = (_dot(xb_ref[hl:hl + tm, :], w_ref[...]) * scale).astype(o_ref.dtype)

    @pl.when(is_conv)
    def _():
        post = jnp.where(j == k_block, ML_HEAD_DIM ** -0.5, 1.0).astype(F32)
        for n, c0 in enumerate(range(0, IN_TN, CONV_COLS)):
            cols = slice(c0, c0 + CONV_COLS)
            u = _dot(xb_ref[...], w_ref[:, cols])
            y = _conv3_rows(u, ub_ref.at[n % 2], cw_ref, cb_ref, cols, tm)
            y = y * jax.nn.sigmoid(y)
            o_ref[:, cols] = (y * post).astype(o_ref.dtype)


def _in_proj(x2, w_merge, w_rest, w_gate, conv_w, conv_b, seq):
    m, d = x2.shape
    n = HP_WIDTH
    tm = min(1024, seq)
    tn = IN_TN
    hl = CONV_HALO
    assert seq % tm == 0 and m % tm == 0 and COL_ML_Q % tn == 0 and COL_ML_K == COL_ML_Q + tn
    assert w_merge.shape[1] == COL_NA_Q and COL_NA_Q % tn == 0 and w_rest.shape[1] >= n - COL_NA_Q
    q_block = COL_ML_Q // tn
    merge_blocks = COL_NA_Q // tn
    rest_blocks = (n - COL_NA_Q) // tn
    nb = tm // hl
    last = m // hl - 1
    conv_idx = lambda i, j: (0, jnp.clip(j - q_block, 0, 1))
    na_blocks = 3 * NA_WIDTH // tn
    ml_blocks = ML_WIDTH // tn
    assert (COL_ML_V, COL_ML_Q) == (COL_NA_Q + 3 * NA_WIDTH, COL_NA_Q + 3 * NA_WIDTH + 2 * ML_WIDTH)

    def rest_src(r):
        return jnp.where(r < na_blocks, r,
                         jnp.where(r < na_blocks + 2 * ml_blocks, r + 2 * ml_blocks, r - 2 * ml_blocks))
    return pl.pallas_call(
        functools.partial(_in_proj_kernel, seq=seq, tm=tm, merge_blocks=merge_blocks,
                          conv_blocks=(q_block, q_block + 1)),
        grid=(m // tm, n // tn),
        in_specs=[
            pl.BlockSpec((tm, d), lambda i, j: (i, 0)),
            pl.BlockSpec((hl, d), lambda i, j: (jnp.maximum(i * nb - 1, 0), 0)),
            pl.BlockSpec((hl, d), lambda i, j: (jnp.minimum((i + 1) * nb, last), 0)),
            pl.BlockSpec((d, tn), lambda i, j: (0, jnp.minimum(j, merge_blocks - 1))),
            pl.BlockSpec((d, tn), lambda i, j: (0, rest_src(jnp.where(j < merge_blocks, rest_blocks - 1,
                                                                      j - merge_blocks)))),
            pl.BlockSpec((d, LANES), lambda i, j: (0, 0)),
            pl.BlockSpec((3, tn), conv_idx),
            pl.BlockSpec((1, tn), conv_idx),
        ],
        out_specs=[
            pl.BlockSpec((tm, tn), lambda i, j: (i, j)),
            pl.BlockSpec((tm, LANES), lambda i, j: (i, 0)),
        ],
        out_shape=[
            jax.ShapeDtypeStruct((m, n), BF16),
            jax.ShapeDtypeStruct((m, LANES), F32),
        ],
        scratch_shapes=[pltpu.VMEM((tm + 2 * hl, d), BF16),
                        pltpu.VMEM((2, tm + 2 * hl, CONV_COLS), F32)],
        compiler_params=pltpu.CompilerParams(
            dimension_semantics=("parallel", "arbitrary"), vmem_limit_bytes=VMEM_LIMIT),
        name="in_proj",
    )(x2, x2, x2, w_merge, w_rest, w_gate, conv_w, conv_b)


def _na_kernel(q_ref, k_ref, v_ref, bias_ref, o_ref, sca_ref, scb_ref, *, rows, group):
    kh = NA_WIN_ROWS
    nkeys = kh * GRID_W
    lane = lax.broadcasted_iota(jnp.int32, (2 * GRID_W, LANES), 1)
    qcol = lax.broadcasted_iota(jnp.int32, (2 * GRID_W, LANES), 0) & (GRID_W - 1)
    kcol = lane & (GRID_W - 1)
    wstart = jnp.clip(qcol - NA_WIN_COLS // 2, 0, GRID_W - NA_WIN_COLS)
    valid = (kcol >= wstart) & (kcol < wstart + NA_WIN_COLS)
    head_a = lax.broadcasted_iota(jnp.int32, (GRID_W, LANES), 1) < NA_HEAD_DIM

    def window(r):
        rs = jnp.clip(r - kh // 2, 0, rows - kh)
        return pl.multiple_of(rs * GRID_W, GRID_W), rs - r + (NA_WIN_ROWS - 1)

    def score_stage(g, sc_ref):
        for u in range(group):
            r = g * group + u
            kbase, _ = window(r)
            q = q_ref[pl.ds(pl.multiple_of(r * GRID_W, GRID_W), GRID_W), :]
            zero = jnp.zeros_like(q)
            qs = jnp.concatenate([jnp.where(head_a, q, zero), jnp.where(head_a, zero, q)], axis=0)
            sc_ref[u] = _dot_nt(qs, k_ref[pl.ds(kbase, nkeys), :])

    def value_stage(g, sc_ref):
        for u in range(group):
            r = g * group + u
            kbase, dr0 = window(r)
            s = []
            for c in range(kh // 2):
                bias = jnp.concatenate([bias_ref[0, dr0 + 2 * c], bias_ref[1, dr0 + 2 * c]], axis=0)
                sc_c = sc_ref[u, :, c * LANES:(c + 1) * LANES] + bias
                s.append(jnp.where(valid, sc_c, NEG_BIG))
            m = jnp.maximum(jnp.maximum(s[0], s[1]), jnp.maximum(s[2], s[3]))
            m = jnp.max(m, axis=-1, keepdims=True)
            p = [jnp.exp2(s_c - m) for s_c in s]
            l = jnp.sum((p[0] + p[1]) + (p[2] + p[3]), axis=-1, keepdims=True)
            pb = jnp.concatenate([p_c.astype(BF16) for p_c in p], axis=1)
            acc = _dot(pb, v_ref[pl.ds(kbase, nkeys), :]) / l
            o = jnp.where(head_a, acc[:GRID_W], acc[GRID_W:])
            o_ref[pl.ds(pl.multiple_of(r * GRID_W, GRID_W), GRID_W), :] = o.astype(o_ref.dtype)

    ngroups = rows // group
    score_stage(0, sca_ref)

    def pair_step(j, carry):
        score_stage(2 * j + 1, scb_ref)
        value_stage(2 * j, sca_ref)
        score_stage(2 * j + 2, sca_ref)
        value_stage(2 * j + 1, scb_ref)
        return carry

    lax.fori_loop(0, ngroups // 2 - 1, pair_step, 0)
    score_stage(ngroups - 1, scb_ref)
    value_stage(ngroups - 2, sca_ref)
    value_stage(ngroups - 1, scb_ref)


def _na_bias_expansion():
    nrel = 2 * NA_WIN_COLS - 1
    qc = np.arange(GRID_W)[:, None]
    kc = np.arange(GRID_W)[None, :]
    dc = np.clip(kc - qc + NA_WIN_COLS - 1, 0, nrel - 1)
    oh = np.zeros((2, nrel, GRID_W, 2, GRID_W), np.float32)
    for half in range(2):
        oh[half, dc, qc, half, kc] = 1.0
    return oh.reshape(2 * nrel, GRID_W * LANES)


def _na_bias_table(rpb):
    r2 = jnp.concatenate([rpb[:, :-1], rpb[:, 1:]], axis=-1)
    r2 = r2.reshape(NA_HEADS * (2 * NA_WIN_ROWS - 2), -1)
    t = jnp.dot(r2, jnp.asarray(_na_bias_expansion()), precision=lax.Precision.HIGHEST)
    return t.reshape(NA_HEADS // 2, 2, 2 * NA_WIN_ROWS - 2, GRID_W, LANES)


def _na_attention(hp3, rpb):
    bsz, s, _ = hp3.shape
    rows = s // GRID_W
    assert s % GRID_W == 0 and rows >= NA_WIN_ROWS
    table = _na_bias_table(rpb.astype(F32) * LOG2E)
    qb, kb, vb = COL_NA_Q // LANES, COL_NA_K // LANES, COL_NA_V // LANES
    group = min(NA_ROW_GROUP, rows // 2)
    assert rows % (2 * group) == 0
    sc_buf = pltpu.VMEM((group, 2 * GRID_W, NA_WIN_ROWS * GRID_W), F32)
    return pl.pallas_call(
        functools.partial(_na_kernel, rows=rows, group=group),
        grid=(bsz, NA_HEADS // 2),
        in_specs=[
            pl.BlockSpec((None, s, LANES), lambda b, h: (b, 0, qb + h)),
            pl.BlockSpec((None, s, LANES), lambda b, h: (b, 0, kb + h)),
            pl.BlockSpec((None, s, LANES), lambda b, h: (b, 0, vb + h)),
            pl.BlockSpec((None, 2, 2 * NA_WIN_ROWS - 2, GRID_W, LANES), lambda b, h: (h, 0, 0, 0, 0)),
        ],
        out_specs=pl.BlockSpec((None, s, LANES), lambda b, h: (b, 0, h)),
        out_shape=jax.ShapeDtypeStruct((bsz, s, NA_WIDTH), BF16),
        scratch_shapes=[sc_buf, sc_buf],
        compiler_params=pltpu.CompilerParams(
            dimension_semantics=("parallel", "parallel"), vmem_limit_bytes=VMEM_LIMIT),
        name="na_attn",
    )(hp3, hp3, hp3, table)


def _log_sigmoid(x):
    return jnp.minimum(x, 0.0) - jnp.log1p(jnp.exp(-jnp.abs(x)))


def _mlstm_kernel(q_ref, k_ref, v_ref, o_ref, g_ref, gt_ref, gbl_ref, gbs_ref, nw_ref, y_ref,
                  hf_ref, hb_ref, c_ref, *, seq):
    L = ML_CHUNK
    nc = seq // L
    head = pl.program_id(1)
    row = lax.broadcasted_iota(jnp.int32, (L, L), 0)
    col = lax.broadcasted_iota(jnp.int32, (L, L), 1)
    tril = row >= col
    triu = row <= col
    ng = 2 * ML_HEADS
    sub8 = lax.broadcasted_iota(jnp.int32, (ng, L), 0)

    c_ref[...] = jnp.zeros_like(c_ref)

    eye = row == col

    def local_scores(d, c):
        t0 = pl.multiple_of(c * L, L)
        q = q_ref[pl.ds(t0, L), :]
        k = k_ref[pl.ds(t0, L), :]
        v = v_ref[pl.ds(t0, L), :]
        return dict(d=d, t0=t0, q=q, k=k, v=v, s_qk=_dot_nt(q, k))

    def local_gates(ch):
        d, t0 = ch["d"], ch["t0"]
        valid = tril if d == 0 else triu
        ich = d * ML_HEADS + head
        pick = sub8 == ich
        gi = gt_ref[0:ng, pl.ds(t0, L)] + gbs_ref[0:ng, :]
        gf = _log_sigmoid(gt_ref[ng:2 * ng, pl.ds(t0, L)] + gbs_ref[ng:2 * ng, :])
        li_row = jnp.sum(jnp.where(pick, gi, 0.0), axis=0, keepdims=True)
        lf_row = jnp.sum(jnp.where(pick, gf, 0.0), axis=0, keepdims=True)
        g_tile = g_ref[pl.ds(t0, L), :] + gbl_ref[...]
        li_col = jnp.sum(jnp.where(col == ich, g_tile, 0.0), axis=1, keepdims=True)
        b_col = jnp.sum(jnp.where(valid, lf_row, 0.0), axis=1, keepdims=True)
        b_row = jnp.sum(jnp.where(eye, b_col, 0.0), axis=0, keepdims=True)
        b_last = jnp.sum(lf_row, axis=1, keepdims=True)
        dlog = jnp.where(valid, b_col - b_row + li_row, NEG_BIG)
        m_loc = jnp.max(dlog, axis=1, keepdims=True)
        a_loc = ch["s_qk"] * jnp.exp(dlog - m_loc)
        logw = b_last - b_col + li_col
        ch.update(b_col=b_col, b_last=b_last, m_loc=m_loc, a_loc=a_loc,
                  rowsum=jnp.sum(a_loc, axis=1, keepdims=True), logw=logw,
                  m_w=jnp.max(logw, axis=0, keepdims=True))

    def state_weights(ch, m_st):
        m_new = jnp.maximum(ch["b_last"] + m_st, ch["m_w"])
        kw = ch["k"].astype(F32) * jnp.exp(ch["logw"] - m_new)
        ch.update(m_in=m_st, decay=jnp.exp(ch["b_last"] + m_st - m_new), kw=kw,
                  ksum=jnp.sum(kw, axis=0, keepdims=True))
        return m_new

    def local_matmuls(ch):
        ch["h_loc"] = _dot(ch["a_loc"].astype(BF16), ch["v"])
        ch["kv"] = _dot_tn(ch["kw"].astype(BF16), ch["v"])

    def state_step(ch, n_st, h_ref):
        d = ch["d"]
        c_st = c_ref[d]
        q = ch["q"]
        qc = _dot(q, c_st.astype(BF16))
        qn = jnp.sum(q.astype(F32) * n_st, axis=1, keepdims=True)
        inter = ch["b_col"] + ch["m_in"]
        m_t = jnp.maximum(ch["m_loc"], inter)
        w_inter = jnp.exp(inter - m_t)
        w_loc = jnp.exp(ch["m_loc"] - m_t)
        den = w_inter * qn + w_loc * ch["rowsum"]
        inv = 1.0 / jnp.maximum(jnp.abs(den), jnp.exp(-m_t))
        h_ref[pl.ds(ch["t0"], L), :] = (w_inter * inv) * qc + (w_loc * inv) * ch["h_loc"]
        c_ref[d] = ch["decay"] * c_st + ch["kv"]
        return ch["decay"] * n_st + ch["ksum"]

    def rec_step(i, carry):
        m_f, n_f, m_b, n_b = carry
        chains = [local_scores(0, 2 * i), local_scores(1, nc - 1 - 2 * i),
                  local_scores(0, 2 * i + 1), local_scores(1, nc - 2 - 2 * i)]
        for ch in chains:
            local_gates(ch)
        m_f = state_weights(chains[0], m_f)
        m_b = state_weights(chains[1], m_b)
        m_f = state_weights(chains[2], m_f)
        m_b = state_weights(chains[3], m_b)
        for ch in chains:
            local_matmuls(ch)
        n_f = state_step(chains[0], n_f, hf_ref)
        n_b = state_step(chains[1], n_b, hb_ref)
        n_f = state_step(chains[2], n_f, hf_ref)
        n_b = state_step(chains[3], n_b, hb_ref)
        return m_f, n_f, m_b, n_b

    assert nc % 2 == 0
    m0 = jnp.full((1, 1), NEG_BIG, F32)
    n0 = jnp.zeros((1, ML_HEAD_DIM), F32)
    lax.fori_loop(0, nc // 2, rec_step, (m0, n0, m0, n0))

    def out_step(c, carry):
        t0 = pl.multiple_of(c * L, L)
        hs = hf_ref[pl.ds(t0, L), :] + hb_ref[pl.ds(t0, L), :]
        mu = jnp.mean(hs, axis=-1, keepdims=True)
        hc = hs - mu
        var = jnp.mean(hc * hc, axis=-1, keepdims=True)
        hn = hc * lax.rsqrt(var + LN_EPS) * nw_ref[...]
        og = jax.nn.sigmoid(o_ref[pl.ds(t0, L), :].astype(F32))
        y_ref[pl.ds(t0, L), :] = (hn * og).astype(y_ref.dtype)
        return carry

    lax.fori_loop(0, nc, out_step, 0, unroll=4)


def _mlstm_branch(hp3, gates3, igate_b, fgate_b, norm_w):
    bsz, s, _ = hp3.shape
    assert s % ML_CHUNK == 0
    hd = ML_HEAD_DIM
    ng = 4 * ML_HEADS
    gates_t = gates3[..., :ng].transpose(0, 2, 1)
    gb = jnp.concatenate([igate_b, fgate_b]).astype(F32)
    gbl = jnp.pad(gb, (0, LANES - ng)).reshape(1, LANES)
    gbs = gb.reshape(ng, 1)
    nw = norm_w.astype(F32).reshape(1, ML_WIDTH)
    qb, kb, vb, ob = COL_ML_Q // hd, COL_ML_K // hd, COL_ML_V // hd, COL_ML_O // hd
    return pl.pallas_call(
        functools.partial(_mlstm_kernel, seq=s),
        grid=(bsz, ML_HEADS),
        in_specs=[
            pl.BlockSpec((None, s, hd), lambda b, h: (b, 0, qb + h)),
            pl.BlockSpec((None, s, hd), lambda b, h: (b, 0, kb + h)),
            pl.BlockSpec((None, s, hd), lambda b, h: (b, 0, vb + h)),
            pl.BlockSpec((None, s, hd), lambda b, h: (b, 0, ob + h)),
            pl.BlockSpec((None, s, LANES), lambda b, h: (b, 0, 0)),
            pl.BlockSpec((None, ng, s), lambda b, h: (b, 0, 0)),
            pl.BlockSpec((1, LANES), lambda b, h: (0, 0)),
            pl.BlockSpec((ng, 1), lambda b, h: (0, 0)),
            pl.BlockSpec((1, hd), lambda b, h: (0, h)),
        ],
        out_specs=pl.BlockSpec((None, s, hd), lambda b, h: (b, 0, h)),
        out_shape=jax.ShapeDtypeStruct((bsz, s, ML_WIDTH), BF16),
        scratch_shapes=[
            pltpu.VMEM((s, hd), F32),
            pltpu.VMEM((s, hd), F32),
            pltpu.VMEM((2, hd, hd), F32),
        ],
        compiler_params=pltpu.CompilerParams(
            dimension_semantics=("parallel", "parallel"), vmem_limit_bytes=VMEM_LIMIT),
        name="mlstm",
    )(hp3, hp3, hp3, hp3, gates3, gates_t, gbl, gbs, nw)


MIX_SUB_ROWS = 256


def _mix_kernel(x_ref, yna_ref, yml_ref, gna_ref, gml_ref, wna_ref, wml_ref, wo_ref, g_ref, b_ref,
                o_ref, *, alpha):
    tm = x_ref.shape[0]
    for r0 in range(0, tm, MIX_SUB_ROWS):
        r = slice(r0, r0 + MIX_SUB_ROWS)
        a = _dot(yna_ref[r, :], wna_ref[...])
        mixed = jax.nn.sigmoid(gna_ref[r, :].astype(F32)) * a
        a = _dot(yml_ref[r, :], wml_ref[...])
        mixed = mixed + jax.nn.sigmoid(gml_ref[r, :].astype(F32)) * a
        z = alpha * x_ref[r, :] + _dot(mixed.astype(BF16), wo_ref[...])
        o_ref[r, :] = _layer_norm(z, g_ref[...], b_ref[...])


def _mix(x2, yna2, yml2, hp, wna, wml, wo, g, b, alpha):
    m, d = x2.shape
    tm = min(512, m)
    assert tm % MIX_SUB_ROWS == 0
    const = dict(pipeline_mode=pl.Buffered(1))
    return pl.pallas_call(
        functools.partial(_mix_kernel, alpha=alpha),
        grid=(m // tm,),
        in_specs=[
            pl.BlockSpec((tm, d), lambda i: (i, 0)),
            pl.BlockSpec((tm, NA_WIDTH), lambda i: (i, 0)),
            pl.BlockSpec((tm, ML_WIDTH), lambda i: (i, 0)),
            pl.BlockSpec((tm, d), lambda i: (i, COL_MERGE_NA // d)),
            pl.BlockSpec((tm, d), lambda i: (i, COL_MERGE_ML // d)),
            pl.BlockSpec((NA_WIDTH, d), lambda i: (0, 0), **const),
            pl.BlockSpec((ML_WIDTH, d), lambda i: (0, 0), **const),
            pl.BlockSpec((d, d), lambda i: (0, 0), **const),
            pl.BlockSpec((1, d), lambda i: (0, 0)),
            pl.BlockSpec((1, d), lambda i: (0, 0)),
        ],
        out_specs=pl.BlockSpec((tm, d), lambda i: (i, 0)),
        out_shape=jax.ShapeDtypeStruct((m, d), F32),
        compiler_params=pltpu.CompilerParams(
            dimension_semantics=("parallel",), vmem_limit_bytes=VMEM_LIMIT),
        name="mix",
    )(x2, yna2, yml2, hp, hp, wna, wml, wo, g, b)


FFN_TF = 512


def _gelu_exact(x):
    return 0.5 * x * (1.0 + lax.erf(x * np.float32(math.sqrt(0.5))))


def _ffn_kernel(x_ref, xp_ref, xn_ref, wg_ref, wv_ref, wd_ref, cw_ref, cb_ref, g_ref, b_ref,
                o_ref, xb_ref, *, alpha, seq, tm, tail):
    i = pl.program_id(0)
    j = pl.program_id(1)
    last = pl.num_programs(1) - 1
    hl = CONV_HALO

    @pl.when(j == 0)
    def _():
        _fill_halo_tile(xb_ref, x_ref, xp_ref, xn_ref, i, tm, seq)
        o_ref[...] = jnp.zeros_like(o_ref)

    def ff_block(width):
        cols = slice(0, width)
        ug = _dot(xb_ref[...], wg_ref[:, cols])
        uv = _dot(xb_ref[hl:hl + tm, :], wv_ref[:, cols])
        c = _conv3_rows(ug, None, cw_ref, cb_ref, cols, tm)
        h = (_gelu_exact(c) * uv).astype(BF16)
        o_ref[...] += _dot(h, wd_ref[cols, :])

    @pl.when(j < last)
    def _():
        ff_block(wg_ref.shape[1])

    @pl.when(j == last)
    def _():
        ff_block(tail)
        z = alpha * x_ref[...] + o_ref[...]
        o_ref[...] = _layer_norm(z, g_ref[...], b_ref[...])


def _ffn(x1, wg, wv, wd, cw, cb, g, b, alpha, seq, dff):
    m, d = x1.shape
    tm = min(1024, seq)
    tf = FFN_TF
    hl = CONV_HALO
    nff = pl.cdiv(dff, tf)
    assert seq % tm == 0 and m % tm == 0 and dff % LANES == 0 and wg.shape == (nff, d, tf)
    nb = tm // hl
    last = m // hl - 1
    return pl.pallas_call(
        functools.partial(_ffn_kernel, alpha=alpha, seq=seq, tm=tm, tail=dff - (nff - 1) * tf),
        grid=(m // tm, nff),
        in_specs=[
            pl.BlockSpec((tm, d), lambda i, j: (i, 0)),
            pl.BlockSpec((hl, d), lambda i, j: (jnp.maximum(i * nb - 1, 0), 0)),
            pl.BlockSpec((hl, d), lambda i, j: (jnp.minimum((i + 1) * nb, last), 0)),
            pl.BlockSpec((None, d, tf), lambda i, j: (j, 0, 0)),
            pl.BlockSpec((None, d, tf), lambda i, j: (j, 0, 0)),
            pl.BlockSpec((tf, d), lambda i, j: (j, 0)),
            pl.BlockSpec((3, tf), lambda i, j: (0, j)),
            pl.BlockSpec((1, tf), lambda i, j: (0, j)),
            pl.BlockSpec((1, d), lambda i, j: (0, 0)),
            pl.BlockSpec((1, d), lambda i, j: (0, 0)),
        ],
        out_specs=pl.BlockSpec((tm, d), lambda i, j: (i, 0)),
        out_shape=jax.ShapeDtypeStruct((m, d), F32),
        scratch_shapes=[pltpu.VMEM((tm + 2 * hl, d), BF16)],
        compiler_params=pltpu.CompilerParams(
            dimension_semantics=("parallel", "arbitrary"), vmem_limit_bytes=VMEM_LIMIT_HIGH),
        name="ffn",
    )(x1, x1, x1, wg, wv, wd, cw, cb, g, b)


def _pad_cols(a, n):
    return jnp.pad(a, ((0, 0), (0, n - a.shape[1])))


def kernel(x, w_in, na_rpb, ml_conv_w, ml_conv_b, ml_igate_b, ml_fgate_b, ml_norm_w, w_branch_na,
           w_branch_ml, w_out, ln1_g, ln1_b, ffn_w_up, ffn_conv_w, ffn_conv_b, ffn_w_down, ln2_g, ln2_b):
    bsz, s, d = x.shape
    depth = w_in.shape[0]
    alpha = float((2 * depth) ** 0.25)
    dff = ffn_conv_w.shape[-1]
    m = bsz * s
    na3 = 3 * NA_WIDTH
    gates_at = na3 + 4 * ML_WIDTH
    merge_at = gates_at + 4 * ML_HEADS
    x2 = x.reshape(m, d)
    for l in range(depth):
        w = w_in[l]
        w_gate = _pad_cols(w[:, gates_at:merge_at], LANES).astype(BF16)
        wb = w.astype(BF16)
        hp, gates = _in_proj(x2, wb[:, merge_at:], wb, w_gate,
                             ml_conv_w[l].astype(F32), ml_conv_b[l].astype(F32).reshape(1, 2 * ML_WIDTH), s)
        hp3 = hp.reshape(bsz, s, HP_WIDTH)
        y_na = _na_attention(hp3, na_rpb[l])
        y_ml = _mlstm_branch(hp3, gates.reshape(bsz, s, LANES), ml_igate_b[l], ml_fgate_b[l],
                             ml_norm_w[l])
        x1 = _mix(x2, y_na.reshape(m, NA_WIDTH), y_ml.reshape(m, ML_WIDTH), hp,
                  w_branch_na[l].astype(BF16), w_branch_ml[l].astype(BF16), w_out[l].astype(BF16),
                  ln1_g[l].reshape(1, d), ln1_b[l].reshape(1, d), alpha)
        dffp = pl.cdiv(dff, FFN_TF) * FFN_TF
        blocked = lambda a: a.reshape(d, dffp // FFN_TF, FFN_TF).transpose(1, 0, 2)
        wg = blocked(_pad_cols(ffn_w_up[l][:, :dff], dffp).astype(BF16))
        wv = blocked(_pad_cols(ffn_w_up[l][:, dff:], dffp).astype(BF16))
        wd = jnp.pad(ffn_w_down[l], ((0, dffp - dff), (0, 0))).astype(BF16)
        cw = _pad_cols(ffn_conv_w[l].astype(F32), dffp)
        cb = _pad_cols(ffn_conv_b[l].astype(F32).reshape(1, dff), dffp)
        x2 = _ffn(x1, wg, wv, wd, cw, cb, ln2_g[l].reshape(1, d), ln2_b[l].reshape(1, d), alpha, s, dff)
    return x2.reshape(bsz, s, d)
```

```python
import functools
import math

import numpy as np
import jax
import jax.numpy as jnp
from jax import lax
from jax.experimental import pallas as pl
from jax.experimental.pallas import tpu as pltpu

GRID_W = 64
NA_HEADS = 16
NA_HEAD_DIM = 64
NA_WIDTH = NA_HEADS * NA_HEAD_DIM
NA_WIN_ROWS = 8
NA_WIN_COLS = 16
NA_ROW_GROUP = 8
LOG2E = math.log2(math.e)
NA_Q_SCALE = NA_HEAD_DIM ** -0.5 * LOG2E
ML_HEADS = 4
ML_HEAD_DIM = 256
ML_WIDTH = ML_HEADS * ML_HEAD_DIM
ML_CHUNK = 128
LN_EPS = 1e-5
NEG_BIG = -1e30

LANES = 128
BF16_ROWS = 16
VMEM_LIMIT = 56 * 1024 * 1024
VMEM_LIMIT_HIGH = 60 * 1024 * 1024

F32 = jnp.float32
BF16 = jnp.bfloat16

COL_MERGE_NA = 0
COL_MERGE_ML = 2048
COL_NA_Q = 4096
COL_NA_K = 5120
COL_NA_V = 6144
COL_ML_V = 7168
COL_ML_O = 8192
COL_ML_Q = 9216
COL_ML_K = 10240
HP_WIDTH = 11264


def _dot(a, b):
    return jnp.dot(a, b, preferred_element_type=F32)


def _dot_nt(a, b):
    return lax.dot_general(a, b, (((1,), (1,)), ((), ())), preferred_element_type=F32)


def _dot_tn(a, b):
    return lax.dot_general(a, b, (((0,), (0,)), ((), ())), preferred_element_type=F32)


def _layer_norm(z, g, b):
    mu = jnp.mean(z, axis=-1, keepdims=True)
    zc = z - mu
    var = jnp.mean(zc * zc, axis=-1, keepdims=True)
    return zc * lax.rsqrt(var + LN_EPS) * g + b


IN_TN = 1024
CONV_HALO = BF16_ROWS
CONV_COLS = 256


def _seq_edge_flags(i, tm, seq):
    has_prev = jnp.where((i * tm) % seq != 0, 1.0, 0.0).astype(F32)
    has_next = jnp.where(((i + 1) * tm) % seq != 0, 1.0, 0.0).astype(F32)
    return has_prev, has_next


def _fill_halo_tile(xb_ref, x_ref, xp_ref, xn_ref, i, tm, seq):
    hl = CONV_HALO
    has_prev, has_next = _seq_edge_flags(i, tm, seq)
    xb_ref[0:hl, :] = (xp_ref[...] * has_prev).astype(BF16)
    xb_ref[hl:hl + tm, :] = x_ref[...].astype(BF16)
    xb_ref[hl + tm:, :] = (xn_ref[...] * has_next).astype(BF16)


def _conv3_rows(u, u_ref, cw_ref, cb_ref, cols, tm):
    hl = CONV_HALO
    if u_ref is None:
        um1 = pltpu.roll(u, 1, 0)[hl:hl + tm]
        u0 = u[hl:hl + tm]
        up1 = pltpu.roll(u, u.shape[0] - 1, 0)[hl:hl + tm]
    else:
        u_ref[...] = u
        um1 = u_ref[hl - 1:hl - 1 + tm, :]
        u0 = u_ref[hl:hl + tm, :]
        up1 = u_ref[hl + 1:hl + 1 + tm, :]
    y = um1 * cw_ref[0:1, cols]
    y = y + u0 * cw_ref[1:2, cols]
    y = y + up1 * cw_ref[2:3, cols]
    return y + cb_ref[:, cols]


def _in_proj_kernel(x_ref, xp_ref, xn_ref, wm_ref, w_ref, wg_ref, cw_ref, cb_ref, o_ref, g_ref, xb_ref,
                    ub_ref, *, seq, tm, merge_blocks, conv_blocks):
    i = pl.program_id(0)
    j = pl.program_id(1)
    hl = CONV_HALO
    q_block, k_block = conv_blocks

    is_merge = j < merge_blocks
    is_conv = (j == q_block) | (j == k_block)

    @pl.when(j == 0)
    def _():
        _fill_halo_tile(xb_ref, x_ref, xp_ref, xn_ref, i, tm, seq)
        g_ref[...] = _dot(xb_ref[hl:hl + tm, :], wg_ref[...])
        o_ref[...] = _dot(xb_ref[hl:hl + tm, :], wm_ref[...]).astype(o_ref.dtype)

    @pl.when((j > 0) & is_merge)
    def _():
        o_ref[...] = _dot(xb_ref[hl:hl + tm, :], wm_ref[...]).astype(o_ref.dtype)

    @pl.when(jnp.logical_not(is_merge | is_conv))
    def _():
        scale = jnp.where(j == COL_NA_Q // IN_TN, NA_Q_SCALE, 1.0).astype(F32)
        o_ref[...] = (_dot(xb_ref[hl:hl + tm, :], w_ref[...]) * scale).astype(o_ref.dtype)

    @pl.when(is_conv)
    def _():
        post = jnp.where(j == k_block, ML_HEAD_DIM ** -0.5, 1.0).astype(F32)
        for n, c0 in enumerate(range(0, IN_TN, CONV_COLS)):
            cols = slice(c0, c0 + CONV_COLS)
            u = _dot(xb_ref[...], w_ref[:, cols])
            y = _conv3_rows(u, ub_ref.at[n % 2], cw_ref, cb_ref, cols, tm)
            y = y * jax.nn.sigmoid(y)
            o_ref[:, cols] = (y * post).astype(o_ref.dtype)


def _in_proj(x2, w_merge, w_rest, w_gate, conv_w, conv_b, seq):
    m, d = x2.shape
    n = HP_WIDTH
    tm = min(1024, seq)
    tn = IN_TN
    hl = CONV_HALO
    assert seq % tm == 0 and m % tm == 0 and COL_ML_Q % tn == 0 and COL_ML_K == COL_ML_Q + tn
    assert w_merge.shape[1] == COL_NA_Q and COL_NA_Q % tn == 0 and w_rest.shape[1] >= n - COL_NA_Q
    q_block = COL_ML_Q // tn
    merge_blocks = COL_NA_Q // tn
    rest_blocks = (n - COL_NA_Q) // tn
    nb = tm // hl
    last = m // hl - 1
    conv_idx = lambda i, j: (0, jnp.clip(j - q_block, 0, 1))
    na_blocks = 3 * NA_WIDTH // tn
    ml_blocks = ML_WIDTH // tn
    assert (COL_ML_V, COL_ML_Q) == (COL_NA_Q + 3 * NA_WIDTH, COL_NA_Q + 3 * NA_WIDTH + 2 * ML_WIDTH)

    def rest_src(r):
        return jnp.where(r < na_blocks, r,
                         jnp.where(r < na_blocks + 2 * ml_blocks, r + 2 * ml_blocks, r - 2 * ml_blocks))
    return pl.pallas_call(
        functools.partial(_in_proj_kernel, seq=seq, tm=tm, merge_blocks=merge_blocks,
                          conv_blocks=(q_block, q_block + 1)),
        grid=(m // tm, n // tn),
        in_specs=[
            pl.BlockSpec((tm, d), lambda i, j: (i, 0)),
            pl.BlockSpec((hl, d), lambda i, j: (jnp.maximum(i * nb - 1, 0), 0)),
            pl.BlockSpec((hl, d), lambda i, j: (jnp.minimum((i + 1) * nb, last), 0)),
            pl.BlockSpec((d, tn), lambda i, j: (0, jnp.minimum(j, merge_blocks - 1))),
            pl.BlockSpec((d, tn), lambda i, j: (0, rest_src(jnp.where(j < merge_blocks, rest_blocks - 1,
                                                                      j - merge_blocks)))),
            pl.BlockSpec((d, LANES), lambda i, j: (0, 0)),
            pl.BlockSpec((3, tn), conv_idx),
            pl.BlockSpec((1, tn), conv_idx),
        ],
        out_specs=[
            pl.BlockSpec((tm, tn), lambda i, j: (i, j)),
            pl.BlockSpec((tm, LANES), lambda i, j: (i, 0)),
        ],
        out_shape=[
            jax.ShapeDtypeStruct((m, n), BF16),
            jax.ShapeDtypeStruct((m, LANES), F32),
        ],
        scratch_shapes=[pltpu.VMEM((tm + 2 * hl, d), BF16),
                        pltpu.VMEM((2, tm + 2 * hl, CONV_COLS), F32)],
        compiler_params=pltpu.CompilerParams(
            dimension_semantics=("parallel", "arbitrary"), vmem_limit_bytes=VMEM_LIMIT),
        name="in_proj",
    )(x2, x2, x2, w_merge, w_rest, w_gate, conv_w, conv_b)


def _na_kernel(q_ref, k_ref, v_ref, bias_ref, o_ref, sca_ref, scb_ref, *, rows, group):
    kh = NA_WIN_ROWS
    nkeys = kh * GRID_W
    lane = lax.broadcasted_iota(jnp.int32, (2 * GRID_W, LANES), 1)
    qcol = lax.broadcasted_iota(jnp.int32, (2 * GRID_W, LANES), 0) & (GRID_W - 1)
    kcol = lane & (GRID_W - 1)
    wstart = jnp.clip(qcol - NA_WIN_COLS // 2, 0, GRID_W - NA_WIN_COLS)
    valid = (kcol >= wstart) & (kcol < wstart + NA_WIN_COLS)
    head_a = lax.broadcasted_iota(jnp.int32, (GRID_W, LANES), 1) < NA_HEAD_DIM

    def window(r):
        rs = jnp.clip(r - kh // 2, 0, rows - kh)
        return pl.multiple_of(rs * GRID_W, GRID_W), rs - r + (NA_WIN_ROWS - 1)

    def score_stage(g, sc_ref):
        for u in range(group):
            r = g * group + u
            kbase, _ = window(r)
            q = q_ref[pl.ds(pl.multiple_of(r * GRID_W, GRID_W), GRID_W), :]
            zero = jnp.zeros_like(q)
            qs = jnp.concatenate([jnp.where(head_a, q, zero), jnp.where(head_a, zero, q)], axis=0)
            sc_ref[u] = _dot_nt(qs, k_ref[pl.ds(kbase, nkeys), :])

    def value_stage(g, sc_ref):
        for u in range(group):
            r = g * group + u
            kbase, dr0 = window(r)
            s = []
            for c in range(kh // 2):
                bias = jnp.concatenate([bias_ref[0, dr0 + 2 * c], bias_ref[1, dr0 + 2 * c]], axis=0)
                sc_c = sc_ref[u, :, c * LANES:(c + 1) * LANES] + bias
                s.append(jnp.where(valid, sc_c, NEG_BIG))
            m = jnp.maximum(jnp.maximum(s[0], s[1]), jnp.maximum(s[2], s[3]))
            m = jnp.max(m, axis=-1, keepdims=True)
            p = [jnp.exp2(s_c - m) for s_c in s]
            l = jnp.sum((p[0] + p[1]) + (p[2] + p[3]), axis=-1, keepdims=True)
            pb = jnp.concatenate([p_c.astype(BF16) for p_c in p], axis=1)
            acc = _dot(pb, v_ref[pl.ds(kbase, nkeys), :]) / l
            o = jnp.where(head_a, acc[:GRID_W], acc[GRID_W:])
            o_ref[pl.ds(pl.multiple_of(r * GRID_W, GRID_W), GRID_W), :] = o.astype(o_ref.dtype)

    ngroups = rows // group
    score_stage(0, sca_ref)

    def pair_step(j, carry):
        score_stage(2 * j + 1, scb_ref)
        value_stage(2 * j, sca_ref)
        score_stage(2 * j + 2, sca_ref)
        value_stage(2 * j + 1, scb_ref)
        return carry

    lax.fori_loop(0, ngroups // 2 - 1, pair_step, 0)
    score_stage(ngroups - 1, scb_ref)
    value_stage(ngroups - 2, sca_ref)
    value_stage(ngroups - 1, scb_ref)


def _na_bias_expansion():
    nrel = 2 * NA_WIN_COLS - 1
    qc = np.arange(GRID_W)[:, None]
    kc = np.arange(GRID_W)[None, :]
    dc = np.clip(kc - qc + NA_WIN_COLS - 1, 0, nrel - 1)
    oh = np.zeros((2, nrel, GRID_W, 2, GRID_W), np.float32)
    for half in range(2):
        oh[half, dc, qc, half, kc] = 1.0
    return oh.reshape(2 * nrel, GRID_W * LANES)


def _na_bias_table(rpb):
    r2 = jnp.concatenate([rpb[:, :-1], rpb[:, 1:]], axis=-1)
    r2 = r2.reshape(NA_HEADS * (2 * NA_WIN_ROWS - 2), -1)
    t = jnp.dot(r2, jnp.asarray(_na_bias_expansion()), precision=lax.Precision.HIGHEST)
    return t.reshape(NA_HEADS // 2, 2, 2 * NA_WIN_ROWS - 2, GRID_W, LANES)


def _na_attention(hp3, rpb):
    bsz, s, _ = hp3.shape
    rows = s // GRID_W
    assert s % GRID_W == 0 and rows >= NA_WIN_ROWS
    table = _na_bias_table(rpb.astype(F32) * LOG2E)
    qb, kb, vb = COL_NA_Q // LANES, COL_NA_K // LANES, COL_NA_V // LANES
    group = min(NA_ROW_GROUP, rows // 2)
    assert rows % (2 * group) == 0
    sc_buf = pltpu.VMEM((group, 2 * GRID_W, NA_WIN_ROWS * GRID_W), F32)
    return pl.pallas_call(
        functools.partial(_na_kernel, rows=rows, group=group),
        grid=(bsz, NA_HEADS // 2),
        in_specs=[
            pl.BlockSpec((None, s, LANES), lambda b, h: (b, 0, qb + h)),
            pl.BlockSpec((None, s, LANES), lambda b, h: (b, 0, kb + h)),
            pl.BlockSpec((None, s, LANES), lambda b, h: (b, 0, vb + h)),
            pl.BlockSpec((None, 2, 2 * NA_WIN_ROWS - 2, GRID_W, LANES), lambda b, h: (h, 0, 0, 0, 0)),
        ],
        out_specs=pl.BlockSpec((None, s, LANES), lambda b, h: (b, 0, h)),
        out_shape=jax.ShapeDtypeStruct((bsz, s, NA_WIDTH), BF16),
        scratch_shapes=[sc_buf, sc_buf],
        compiler_params=pltpu.CompilerParams(
            dimension_semantics=("parallel", "parallel"), vmem_limit_bytes=VMEM_LIMIT),
        name="na_attn",
    )(hp3, hp3, hp3, table)


def _log_sigmoid(x):
    return jnp.minimum(x, 0.0) - jnp.log1p(jnp.exp(-jnp.abs(x)))


def _mlstm_kernel(q_ref, k_ref, v_ref, o_ref, g_ref, gt_ref, gbl_ref, gbs_ref, nw_ref, y_ref,
                  hf_ref, hb_ref, c_ref, *, seq):
    L = ML_CHUNK
    nc = seq // L
    head = pl.program_id(1)
    row = lax.broadcasted_iota(jnp.int32, (L, L), 0)
    col = lax.broadcasted_iota(jnp.int32, (L, L), 1)
    tril = row >= col
    triu = row <= col
    ng = 2 * ML_HEADS
    sub8 = lax.broadcasted_iota(jnp.int32, (ng, L), 0)

    c_ref[...] = jnp.zeros_like(c_ref)

    eye = row == col

    def local_scores(d, c):
        t0 = pl.multiple_of(c * L, L)
        q = q_ref[pl.ds(t0, L), :]
        k = k_ref[pl.ds(t0, L), :]
        v = v_ref[pl.ds(t0, L), :]
        return dict(d=d, t0=t0, q=q, k=k, v=v, s_qk=_dot_nt(q, k))

    def local_gates(ch):
        d, t0 = ch["d"], ch["t0"]
        valid = tril if d == 0 else triu
        ich = d * ML_HEADS + head
        pick = sub8 == ich
        gi = gt_ref[0:ng, pl.ds(t0, L)] + gbs_ref[0:ng, :]
        gf = _log_sigmoid(gt_ref[ng:2 * ng, pl.ds(t0, L)] + gbs_ref[ng:2 * ng, :])
        li_row = jnp.sum(jnp.where(pick, gi, 0.0), axis=0, keepdims=True)
        lf_row = jnp.sum(jnp.where(pick, gf, 0.0), axis=0, keepdims=True)
        g_tile = g_ref[pl.ds(t0, L), :] + gbl_ref[...]
        li_col = jnp.sum(jnp.where(col == ich, g_tile, 0.0), axis=1, keepdims=True)
        b_col = jnp.sum(jnp.where(valid, lf_row, 0.0), axis=1, keepdims=True)
        b_row = jnp.sum(jnp.where(eye, b_col, 0.0), axis=0, keepdims=True)
        b_last = jnp.sum(lf_row, axis=1, keepdims=True)
        dlog = jnp.where(valid, b_col - b_row + li_row, NEG_BIG)
        m_loc = jnp.max(dlog, axis=1, keepdims=True)
        a_loc = ch["s_qk"] * jnp.exp(dlog - m_loc)
        logw = b_last - b_col + li_col
        ch.update(b_col=b_col, b_last=b_last, m_loc=m_loc, a_loc=a_loc,
                  rowsum=jnp.sum(a_loc, axis=1, keepdims=True), logw=logw,
                  m_w=jnp.max(logw, axis=0, keepdims=True))

    def state_weights(ch, m_st):
        m_new = jnp.maximum(ch["b_last"] + m_st, ch["m_w"])
        kw = ch["k"].astype(F32) * jnp.exp(ch["logw"] - m_new)
        ch.update(m_in=m_st, decay=jnp.exp(ch["b_last"] + m_st - m_new), kw=kw,
                  ksum=jnp.sum(kw, axis=0, keepdims=True))
        return m_new

    def local_matmuls(ch):
        ch["h_loc"] = _dot(ch["a_loc"].astype(BF16), ch["v"])
        ch["kv"] = _dot_tn(ch["kw"].astype(BF16), ch["v"])

    def state_step(ch, n_st, h_ref):
        d = ch["d"]
        c_st = c_ref[d]
        q = ch["q"]
        qc = _dot(q, c_st.astype(BF16))
        qn = jnp.sum(q.astype(F32) * n_st, axis=1, keepdims=True)
        inter = ch["b_col"] + ch["m_in"]
        m_t = jnp.maximum(ch["m_loc"], inter)
        w_inter = jnp.exp(inter - m_t)
        w_loc = jnp.exp(ch["m_loc"] - m_t)
        den = w_inter * qn + w_loc * ch["rowsum"]
        inv = 1.0 / jnp.maximum(jnp.abs(den), jnp.exp(-m_t))
        h_ref[pl.ds(ch["t0"], L), :] = (w_inter * inv) * qc + (w_loc * inv) * ch["h_loc"]
        c_ref[d] = ch["decay"] * c_st + ch["kv"]
        return ch["decay"] * n_st + ch["ksum"]

    def rec_step(i, carry):
        m_f, n_f, m_b, n_b = carry
        chains = [local_scores(0, 2 * i), local_scores(1, nc - 1 - 2 * i),
                  local_scores(0, 2 * i + 1), local_scores(1, nc - 2 - 2 * i)]
        for ch in chains:
            local_gates(ch)
        m_f = state_weights(chains[0], m_f)
        m_b = state_weights(chains[1], m_b)
        m_f = state_weights(chains[2], m_f)
        m_b = state_weights(chains[3], m_b)
        for ch in chains:
            local_matmuls(ch)
        n_f = state_step(chains[0], n_f, hf_ref)
        n_b = state_step(chains[1], n_b, hb_ref)
        n_f = state_step(chains[2], n_f, hf_ref)
        n_b = state_step(chains[3], n_b, hb_ref)
        return m_f, n_f, m_b, n_b

    assert nc % 2 == 0
    m0 = jnp.full((1, 1), NEG_BIG, F32)
    n0 = jnp.zeros((1, ML_HEAD_DIM), F32)
    lax.fori_loop(0, nc // 2, rec_step, (m0, n0, m0, n0))

    def out_step(c, carry):
        t0 = pl.multiple_of(c * L, L)
        hs = hf_ref[pl.ds(t0, L), :] + hb_ref[pl.ds(t0, L), :]
        mu = jnp.mean(hs, axis=-1, keepdims=True)
        hc = hs - mu
        var = jnp.mean(hc * hc, axis=-1, keepdims=True)
        hn = hc * lax.rsqrt(var + LN_EPS) * nw_ref[...]
        og = jax.nn.sigmoid(o_ref[pl.ds(t0, L), :].astype(F32))
        y_ref[pl.ds(t0, L), :] = (hn * og).astype(y_ref.dtype)
        return carry

    lax.fori_loop(0, nc, out_step, 0, unroll=4)


def _mlstm_branch(hp3, gates3, igate_b, fgate_b, norm_w):
    bsz, s, _ = hp3.shape
    assert s % ML_CHUNK == 0
    hd = ML_HEAD_DIM
    ng = 4 * ML_HEADS
    gates_t = gates3[..., :ng].transpose(0, 2, 1)
    gb = jnp.concatenate([igate_b, fgate_b]).astype(F32)
    gbl = jnp.pad(gb, (0, LANES - ng)).reshape(1, LANES)
    gbs = gb.reshape(ng, 1)
    nw = norm_w.astype(F32).reshape(1, ML_WIDTH)
    qb, kb, vb, ob = COL_ML_Q // hd, COL_ML_K // hd, COL_ML_V // hd, COL_ML_O // hd
    return pl.pallas_call(
        functools.partial(_mlstm_kernel, seq=s),
        grid=(bsz, ML_HEADS),
        in_specs=[
            pl.BlockSpec((None, s, hd), lambda b, h: (b, 0, qb + h)),
            pl.BlockSpec((None, s, hd), lambda b, h: (b, 0, kb + h)),
            pl.BlockSpec((None, s, hd), lambda b, h: (b, 0, vb + h)),
            pl.BlockSpec((None, s, hd), lambda b, h: (b, 0, ob + h)),
            pl.BlockSpec((None, s, LANES), lambda b, h: (b, 0, 0)),
            pl.BlockSpec((None, ng, s), lambda b, h: (b, 0, 0)),
            pl.BlockSpec((1, LANES), lambda b, h: (0, 0)),
            pl.BlockSpec((ng, 1), lambda b, h: (0, 0)),
            pl.BlockSpec((1, hd), lambda b, h: (0, h)),
        ],
        out_specs=pl.BlockSpec((None, s, hd), lambda b, h: (b, 0, h)),
        out_shape=jax.ShapeDtypeStruct((bsz, s, ML_WIDTH), BF16),
        scratch_shapes=[
            pltpu.VMEM((s, hd), F32),
            pltpu.VMEM((s, hd), F32),
            pltpu.VMEM((2, hd, hd), F32),
        ],
        compiler_params=pltpu.CompilerParams(
            dimension_semantics=("parallel", "parallel"), vmem_limit_bytes=VMEM_LIMIT),
        name="mlstm",
    )(hp3, hp3, hp3, hp3, gates3, gates_t, gbl, gbs, nw)


MIX_SUB_ROWS = 256


def _mix_kernel(x_ref, yna_ref, yml_ref, gna_ref, gml_ref, wna_ref, wml_ref, wo_ref, g_ref, b_ref,
                o_ref, *, alpha):
    tm = x_ref.shape[0]
    for r0 in range(0, tm, MIX_SUB_ROWS):
        r = slice(r0, r0 + MIX_SUB_ROWS)
        a = _dot(yna_ref[r, :], wna_ref[...])
        mixed = jax.nn.sigmoid(gna_ref[r, :].astype(F32)) * a
        a = _dot(yml_ref[r, :], wml_ref[...])
        mixed = mixed + jax.nn.sigmoid(gml_ref[r, :].astype(F32)) * a
        z = alpha * x_ref[r, :] + _dot(mixed.astype(BF16), wo_ref[...])
        o_ref[r, :] = _layer_norm(z, g_ref[...], b_ref[...])


def _mix(x2, yna2, yml2, hp, wna, wml, wo, g, b, alpha):
    m, d = x2.shape
    tm = min(512, m)
    assert tm % MIX_SUB_ROWS == 0
    const = dict(pipeline_mode=pl.Buffered(1))
    return pl.pallas_call(
        functools.partial(_mix_kernel, alpha=alpha),
        grid=(m // tm,),
        in_specs=[
            pl.BlockSpec((tm, d), lambda i: (i, 0)),
            pl.BlockSpec((tm, NA_WIDTH), lambda i: (i, 0)),
            pl.BlockSpec((tm, ML_WIDTH), lambda i: (i, 0)),
            pl.BlockSpec((tm, d), lambda i: (i, COL_MERGE_NA // d)),
            pl.BlockSpec((tm, d), lambda i: (i, COL_MERGE_ML // d)),
            pl.BlockSpec((NA_WIDTH, d), lambda i: (0, 0), **const),
            pl.BlockSpec((ML_WIDTH, d), lambda i: (0, 0), **const),
            pl.BlockSpec((d, d), lambda i: (0, 0), **const),
            pl.BlockSpec((1, d), lambda i: (0, 0)),
            pl.BlockSpec((1, d), lambda i: (0, 0)),
        ],
        out_specs=pl.BlockSpec((tm, d), lambda i: (i, 0)),
        out_shape=jax.ShapeDtypeStruct((m, d), F32),
        compiler_params=pltpu.CompilerParams(
            dimension_semantics=("parallel",), vmem_limit_bytes=VMEM_LIMIT),
        name="mix",
    )(x2, yna2, yml2, hp, hp, wna, wml, wo, g, b)


FFN_TF = 512
FFN_LN_ROWS = 256


def _gelu_exact(x):
    return 0.5 * x * (1.0 + lax.erf(x * np.float32(math.sqrt(0.5))))


def _ffn_kernel(x_ref, xp_ref, xn_ref, wg_ref, wv_ref, wd_ref, cw_ref, cb_ref, g_ref, b_ref,
                o_ref, xb_ref, *, alpha, seq, tm, tail):
    i = pl.program_id(0)
    j = pl.program_id(1)
    last = pl.num_programs(1) - 1
    hl = CONV_HALO

    def glu_block(width):
        cols = slice(0, width)
        ug = _dot(xb_ref[...], wg_ref[:, cols])
        uv = _dot(xb_ref[hl:hl + tm, :], wv_ref[:, cols])
        c = _conv3_rows(ug, None, cw_ref, cb_ref, cols, tm)
        return (_gelu_exact(c) * uv).astype(BF16), cols

    @pl.when(j == 0)
    def _():
        _fill_halo_tile(xb_ref, x_ref, xp_ref, xn_ref, i, tm, seq)
        h, cols = glu_block(wg_ref.shape[1])
        o_ref[...] = _dot(h, wd_ref[cols, :])

    @pl.when((j > 0) & (j < last))
    def _():
        h, cols = glu_block(wg_ref.shape[1])
        o_ref[...] += _dot(h, wd_ref[cols, :])

    @pl.when(j == last)
    def _():
        h, cols = glu_block(tail)
        for r0 in range(0, tm, FFN_LN_ROWS):
            r = slice(r0, r0 + FFN_LN_ROWS)
            z = alpha * x_ref[r, :] + (o_ref[r, :] + _dot(h[r], wd_ref[cols, :]))
            o_ref[r, :] = _layer_norm(z, g_ref[...], b_ref[...])


def _ffn(x1, wg, wv, wd, cw, cb, g, b, alpha, seq, dff):
    m, d = x1.shape
    tm = min(1024, seq)
    tf = FFN_TF
    hl = CONV_HALO
    nff = pl.cdiv(dff, tf)
    assert seq % tm == 0 and m % tm == 0 and dff % LANES == 0 and wg.shape == (d, nff * tf) and nff >= 2
    nb = tm // hl
    last = m // hl - 1
    return pl.pallas_call(
        functools.partial(_ffn_kernel, alpha=alpha, seq=seq, tm=tm, tail=dff - (nff - 1) * tf),
        grid=(m // tm, nff),
        in_specs=[
            pl.BlockSpec((tm, d), lambda i, j: (i, 0)),
            pl.BlockSpec((hl, d), lambda i, j: (jnp.maximum(i * nb - 1, 0), 0)),
            pl.BlockSpec((hl, d), lambda i, j: (jnp.minimum((i + 1) * nb, last), 0)),
            pl.BlockSpec((d, tf), lambda i, j: (0, j)),
            pl.BlockSpec((d, tf), lambda i, j: (0, j)),
            pl.BlockSpec((tf, d), lambda i, j: (j, 0)),
            pl.BlockSpec((3, tf), lambda i, j: (0, j)),
            pl.BlockSpec((1, tf), lambda i, j: (0, j)),
            pl.BlockSpec((1, d), lambda i, j: (0, 0)),
            pl.BlockSpec((1, d), lambda i, j: (0, 0)),
        ],
        out_specs=pl.BlockSpec((tm, d), lambda i, j: (i, 0)),
        out_shape=jax.ShapeDtypeStruct((m, d), F32),
        scratch_shapes=[pltpu.VMEM((tm + 2 * hl, d), BF16)],
        compiler_params=pltpu.CompilerParams(
            dimension_semantics=("parallel", "arbitrary"), vmem_limit_bytes=VMEM_LIMIT_HIGH),
        name="ffn",
    )(x1, x1, x1, wg, wv, wd, cw, cb, g, b)


def _pad_cols(a, n):
    return jnp.pad(a, ((0, 0), (0, n - a.shape[1])))


def kernel(x, w_in, na_rpb, ml_conv_w, ml_conv_b, ml_igate_b, ml_fgate_b, ml_norm_w, w_branch_na,
           w_branch_ml, w_out, ln1_g, ln1_b, ffn_w_up, ffn_conv_w, ffn_conv_b, ffn_w_down, ln2_g, ln2_b):
    bsz, s, d = x.shape
    depth = w_in.shape[0]
    alpha = float((2 * depth) ** 0.25)
    dff = ffn_conv_w.shape[-1]
    m = bsz * s
    na3 = 3 * NA_WIDTH
    gates_at = na3 + 4 * ML_WIDTH
    merge_at = gates_at + 4 * ML_HEADS
    x2 = x.reshape(m, d)
    for l in range(depth):
        w = w_in[l]
        w_gate = _pad_cols(w[:, gates_at:merge_at], LANES).astype(BF16)
        wb = w.astype(BF16)
        hp, gates = _in_proj(x2, wb[:, merge_at:], wb, w_gate,
                             ml_conv_w[l].astype(F32), ml_conv_b[l].astype(F32).reshape(1, 2 * ML_WIDTH), s)
        hp3 = hp.reshape(bsz, s, HP_WIDTH)
        y_na = _na_attention(hp3, na_rpb[l])
        y_ml = _mlstm_branch(hp3, gates.reshape(bsz, s, LANES), ml_igate_b[l], ml_fgate_b[l],
                             ml_norm_w[l])
        x1 = _mix(x2, y_na.reshape(m, NA_WIDTH), y_ml.reshape(m, ML_WIDTH), hp,
                  w_branch_na[l].astype(BF16), w_branch_ml[l].astype(BF16), w_out[l].astype(BF16),
                  ln1_g[l].reshape(1, d), ln1_b[l].reshape(1, d), alpha)
        dffp = pl.cdiv(dff, FFN_TF) * FFN_TF
        wg = _pad_cols(ffn_w_up[l][:, :dff], dffp).astype(BF16)
        wv = _pad_cols(ffn_w_up[l][:, dff:], dffp).astype(BF16)
        wd = jnp.pad(ffn_w_down[l], ((0, dffp - dff), (0, 0))).astype(BF16)
        cw = _pad_cols(ffn_conv_w[l].astype(F32), dffp)
        cb = _pad_cols(ffn_conv_b[l].astype(F32).reshape(1, dff), dffp)
        x2 = _ffn(x1, wg, wv, wd, cw, cb, ln2_g[l].reshape(1, d), ln2_b[l].reshape(1, d), alpha, s, dff)
    return x2.reshape(bsz, s, d)
```

```python
import functools
import math

import numpy as np
import jax
import jax.numpy as jnp
from jax import lax
from jax.experimental import pallas as pl
from jax.experimental.pallas import tpu as pltpu

GRID_W = 64
NA_HEADS = 16
NA_HEAD_DIM = 64
NA_WIDTH = NA_HEADS * NA_HEAD_DIM
NA_WIN_ROWS = 8
NA_WIN_COLS = 16
NA_ROW_GROUP = 8
LOG2E = math.log2(math.e)
NA_Q_SCALE = NA_HEAD_DIM ** -0.5 * LOG2E
ML_HEADS = 4
ML_HEAD_DIM = 256
ML_WIDTH = ML_HEADS * ML_HEAD_DIM
ML_CHUNK = 128
ML_CHUNKS_PER_STEP = 2
LN_EPS = 1e-5
NEG_BIG = -1e30

LANES = 128
BF16_ROWS = 16
VMEM_LIMIT = 56 * 1024 * 1024
VMEM_LIMIT_HIGH = 60 * 1024 * 1024

F32 = jnp.float32
BF16 = jnp.bfloat16

COL_MERGE_NA = 0
COL_MERGE_ML = 2048
COL_NA_Q = 4096
COL_NA_K = 5120
COL_NA_V = 6144
COL_ML_V = 7168
COL_ML_O = 8192
COL_ML_Q = 9216
COL_ML_K = 10240
HP_WIDTH = 11264


def _dot(a, b):
    return jnp.dot(a, b, preferred_element_type=F32)


def _dot_nt(a, b):
    return lax.dot_general(a, b, (((1,), (1,)), ((), ())), preferred_element_type=F32)


def _dot_tn(a, b):
    return lax.dot_general(a, b, (((0,), (0,)), ((), ())), preferred_element_type=F32)


def _layer_norm(z, g, b):
    mu = jnp.mean(z, axis=-1, keepdims=True)
    zc = z - mu
    var = jnp.mean(zc * zc, axis=-1, keepdims=True)
    return zc * lax.rsqrt(var + LN_EPS) * g + b


IN_TN = 1024
CONV_HALO = BF16_ROWS
CONV_COLS = 256


def _seq_edge_flags(i, tm, seq):
    has_prev = jnp.where((i * tm) % seq != 0, 1.0, 0.0).astype(F32)
    has_next = jnp.where(((i + 1) * tm) % seq != 0, 1.0, 0.0).astype(F32)
    return has_prev, has_next


def _fill_halo_tile(xb_ref, x_ref, xp_ref, xn_ref, i, tm, seq):
    hl = CONV_HALO
    has_prev, has_next = _seq_edge_flags(i, tm, seq)
    xb_ref[0:hl, :] = (xp_ref[...] * has_prev).astype(BF16)
    xb_ref[hl:hl + tm, :] = x_ref[...].astype(BF16)
    xb_ref[hl + tm:, :] = (xn_ref[...] * has_next).astype(BF16)


def _conv3_rows(u, u_ref, cw_ref, cb_ref, cols, tm):
    hl = CONV_HALO
    if u_ref is None:
        um1 = pltpu.roll(u, 1, 0)[hl:hl + tm]
        u0 = u[hl:hl + tm]
        up1 = pltpu.roll(u, u.shape[0] - 1, 0)[hl:hl + tm]
    else:
        u_ref[...] = u
        um1 = u_ref[hl - 1:hl - 1 + tm, :]
        u0 = u_ref[hl:hl + tm, :]
        up1 = u_ref[hl + 1:hl + 1 + tm, :]
    y = um1 * cw_ref[0:1, cols]
    y = y + u0 * cw_ref[1:2, cols]
    y = y + up1 * cw_ref[2:3, cols]
    return y + cb_ref[:, cols]


def _in_proj_kernel(x_ref, xp_ref, xn_ref, wm_ref, w_ref, wg_ref, cw_ref, cb_ref, o_ref, g_ref, xb_ref,
                    ub_ref, *, seq, tm, merge_blocks, conv_blocks):
    i = pl.program_id(0)
    j = pl.program_id(1)
    hl = CONV_HALO
    q_block, k_block = conv_blocks

    is_merge = j < merge_blocks
    is_conv = (j == q_block) | (j == k_block)

    @pl.when(j == 0)
    def _():
        _fill_halo_tile(xb_ref, x_ref, xp_ref, xn_ref, i, tm, seq)
        g_ref[...] = _dot(xb_ref[hl:hl + tm, :], wg_ref[...])
        o_ref[...] = _dot(xb_ref[hl:hl + tm, :], wm_ref[...]).astype(o_ref.dtype)

    @pl.when((j > 0) & is_merge)
    def _():
        o_ref[...] = _dot(xb_ref[hl:hl + tm, :], wm_ref[...]).astype(o_ref.dtype)

    @pl.when(jnp.logical_not(is_merge | is_conv))
    def _():
        scale = jnp.where(j == COL_NA_Q // IN_TN, NA_Q_SCALE, 1.0).astype(F32)
        o_ref[...] = (_dot(xb_ref[hl:hl + tm, :], w_ref[...]) * scale).astype(o_ref.dtype)

    @pl.when(is_conv)
    def _():
        post = jnp.where(j == k_block, ML_HEAD_DIM ** -0.5, 1.0).astype(F32)
        for n, c0 in enumerate(range(0, IN_TN, CONV_COLS)):
            cols = slice(c0, c0 + CONV_COLS)
            u = _dot(xb_ref[...], w_ref[:, cols])
            y = _conv3_rows(u, ub_ref.at[n % 2], cw_ref, cb_ref, cols, tm)
            y = y * jax.nn.sigmoid(y)
            o_ref[:, cols] = (y * post).astype(o_ref.dtype)


def _in_proj(x2, w_merge, w_rest, w_gate, conv_w, conv_b, seq):
    m, d = x2.shape
    n = HP_WIDTH
    tm = min(1024, seq)
    tn = IN_TN
    hl = CONV_HALO
    assert seq % tm == 0 and m % tm == 0 and COL_ML_Q % tn == 0 and COL_ML_K == COL_ML_Q + tn
    assert w_merge.shape[1] == COL_NA_Q and COL_NA_Q % tn == 0 and w_rest.shape[1] >= n - COL_NA_Q
    q_block = COL_ML_Q // tn
    merge_blocks = COL_NA_Q // tn
    rest_blocks = (n - COL_NA_Q) // tn
    nb = tm // hl
    last = m // hl - 1
    conv_idx = lambda i, j: (0, jnp.clip(j - q_block, 0, 1))
    na_blocks = 3 * NA_WIDTH // tn
    ml_blocks = ML_WIDTH // tn
    assert (COL_ML_V, COL_ML_Q) == (COL_NA_Q + 3 * NA_WIDTH, COL_NA_Q + 3 * NA_WIDTH + 2 * ML_WIDTH)

    def rest_src(r):
        return jnp.where(r < na_blocks, r,
                         jnp.where(r < na_blocks + 2 * ml_blocks, r + 2 * ml_blocks, r - 2 * ml_blocks))
    return pl.pallas_call(
        functools.partial(_in_proj_kernel, seq=seq, tm=tm, merge_blocks=merge_blocks,
                          conv_blocks=(q_block, q_block + 1)),
        grid=(m // tm, n // tn),
        in_specs=[
            pl.BlockSpec((tm, d), lambda i, j: (i, 0)),
            pl.BlockSpec((hl, d), lambda i, j: (jnp.maximum(i * nb - 1, 0), 0)),
            pl.BlockSpec((hl, d), lambda i, j: (jnp.minimum((i + 1) * nb, last), 0)),
            pl.BlockSpec((d, tn), lambda i, j: (0, jnp.minimum(j, merge_blocks - 1))),
            pl.BlockSpec((d, tn), lambda i, j: (0, rest_src(jnp.where(j < merge_blocks, rest_blocks - 1,
                                                                      j - merge_blocks)))),
            pl.BlockSpec((d, LANES), lambda i, j: (0, 0)),
            pl.BlockSpec((3, tn), conv_idx),
            pl.BlockSpec((1, tn), conv_idx),
        ],
        out_specs=[
            pl.BlockSpec((tm, tn), lambda i, j: (i, j)),
            pl.BlockSpec((tm, LANES), lambda i, j: (i, 0)),
        ],
        out_shape=[
            jax.ShapeDtypeStruct((m, n), BF16),
            jax.ShapeDtypeStruct((m, LANES), F32),
        ],
        scratch_shapes=[pltpu.VMEM((tm + 2 * hl, d), BF16),
                        pltpu.VMEM((2, tm + 2 * hl, CONV_COLS), F32)],
        compiler_params=pltpu.CompilerParams(
            dimension_semantics=("parallel", "arbitrary"), vmem_limit_bytes=VMEM_LIMIT),
        name="in_proj",
    )(x2, x2, x2, w_merge, w_rest, w_gate, conv_w, conv_b)


def _na_kernel(q_ref, k_ref, v_ref, bias_ref, o_ref, sca_ref, scb_ref, *, rows, group):
    kh = NA_WIN_ROWS
    nkeys = kh * GRID_W
    lane = lax.broadcasted_iota(jnp.int32, (2 * GRID_W, LANES), 1)
    qcol = lax.broadcasted_iota(jnp.int32, (2 * GRID_W, LANES), 0) & (GRID_W - 1)
    kcol = lane & (GRID_W - 1)
    wstart = jnp.clip(qcol - NA_WIN_COLS // 2, 0, GRID_W - NA_WIN_COLS)
    valid = (kcol >= wstart) & (kcol < wstart + NA_WIN_COLS)
    head_a = lax.broadcasted_iota(jnp.int32, (GRID_W, LANES), 1) < NA_HEAD_DIM

    def window(r):
        rs = min(max(r - kh // 2, 0), rows - kh)
        return rs * GRID_W, rs - r + (NA_WIN_ROWS - 1)

    def score_stage(g, sc_ref):
        for u in range(group):
            r = g * group + u
            kbase, _ = window(r)
            q = q_ref[pl.ds(r * GRID_W, GRID_W), :]
            zero = jnp.zeros_like(q)
            qs = jnp.concatenate([jnp.where(head_a, q, zero), jnp.where(head_a, zero, q)], axis=0)
            sc_ref[u] = _dot_nt(qs, k_ref[pl.ds(kbase, nkeys), :])

    def value_stage(g, sc_ref):
        for u in range(group):
            r = g * group + u
            kbase, dr0 = window(r)
            s = []
            for c in range(kh // 2):
                bias = jnp.concatenate([bias_ref[0, dr0 + 2 * c], bias_ref[1, dr0 + 2 * c]], axis=0)
                sc_c = sc_ref[u, :, c * LANES:(c + 1) * LANES] + bias
                s.append(jnp.where(valid, sc_c, NEG_BIG))
            m = jnp.maximum(jnp.maximum(s[0], s[1]), jnp.maximum(s[2], s[3]))
            m = jnp.max(m, axis=-1, keepdims=True)
            p = [jnp.exp2(s_c - m) for s_c in s]
            l = jnp.sum((p[0] + p[1]) + (p[2] + p[3]), axis=-1, keepdims=True)
            pb = jnp.concatenate([p_c.astype(BF16) for p_c in p], axis=1)
            acc = _dot(pb, v_ref[pl.ds(kbase, nkeys), :]) / l
            o = jnp.where(head_a, acc[:GRID_W], acc[GRID_W:])
            o_ref[pl.ds(r * GRID_W, GRID_W), :] = o.astype(o_ref.dtype)

    ngroups = rows // group
    bufs = (sca_ref, scb_ref)
    score_stage(0, bufs[0])
    for g in range(ngroups):
        if g + 1 < ngroups:
            score_stage(g + 1, bufs[(g + 1) % 2])
        value_stage(g, bufs[g % 2])


def _na_bias_expansion():
    nrel = 2 * NA_WIN_COLS - 1
    qc = np.arange(GRID_W)[:, None]
    kc = np.arange(GRID_W)[None, :]
    dc = np.clip(kc - qc + NA_WIN_COLS - 1, 0, nrel - 1)
    oh = np.zeros((2, nrel, GRID_W, 2, GRID_W), np.float32)
    for half in range(2):
        oh[half, dc, qc, half, kc] = 1.0
    return oh.reshape(2 * nrel, GRID_W * LANES)


def _na_bias_table(rpb):
    r2 = jnp.concatenate([rpb[:, :-1], rpb[:, 1:]], axis=-1)
    r2 = r2.reshape(NA_HEADS * (2 * NA_WIN_ROWS - 2), -1)
    t = jnp.dot(r2, jnp.asarray(_na_bias_expansion()), precision=lax.Precision.HIGHEST)
    return t.reshape(NA_HEADS // 2, 2, 2 * NA_WIN_ROWS - 2, GRID_W, LANES)


def _na_attention(hp3, rpb):
    bsz, s, _ = hp3.shape
    rows = s // GRID_W
    assert s % GRID_W == 0 and rows >= NA_WIN_ROWS
    table = _na_bias_table(rpb.astype(F32) * LOG2E)
    qb, kb, vb = COL_NA_Q // LANES, COL_NA_K // LANES, COL_NA_V // LANES
    group = min(NA_ROW_GROUP, rows // 2)
    assert rows % (2 * group) == 0
    sc_buf = pltpu.VMEM((group, 2 * GRID_W, NA_WIN_ROWS * GRID_W), F32)
    return pl.pallas_call(
        functools.partial(_na_kernel, rows=rows, group=group),
        grid=(bsz, NA_HEADS // 2),
        in_specs=[
            pl.BlockSpec((None, s, LANES), lambda b, h: (b, 0, qb + h)),
            pl.BlockSpec((None, s, LANES), lambda b, h: (b, 0, kb + h)),
            pl.BlockSpec((None, s, LANES), lambda b, h: (b, 0, vb + h)),
            pl.BlockSpec((None, 2, 2 * NA_WIN_ROWS - 2, GRID_W, LANES), lambda b, h: (h, 0, 0, 0, 0)),
        ],
        out_specs=pl.BlockSpec((None, s, LANES), lambda b, h: (b, 0, h)),
        out_shape=jax.ShapeDtypeStruct((bsz, s, NA_WIDTH), BF16),
        scratch_shapes=[sc_buf, sc_buf],
        compiler_params=pltpu.CompilerParams(
            dimension_semantics=("parallel", "parallel"), vmem_limit_bytes=VMEM_LIMIT),
        name="na_attn",
    )(hp3, hp3, hp3, table)


def _log_sigmoid(x):
    return jnp.minimum(x, 0.0) - jnp.log1p(jnp.exp(-jnp.abs(x)))


def _mlstm_kernel(q_ref, k_ref, v_ref, o_ref, g_ref, gt_ref, gbl_ref, gbs_ref, nw_ref, y_ref,
                  hf_ref, hb_ref, c_ref, *, seq):
    L = ML_CHUNK
    nc = seq // L
    head = pl.program_id(1)
    row = lax.broadcasted_iota(jnp.int32, (L, L), 0)
    col = lax.broadcasted_iota(jnp.int32, (L, L), 1)
    tril = row >= col
    triu = row <= col
    ng = 2 * ML_HEADS
    sub8 = lax.broadcasted_iota(jnp.int32, (ng, L), 0)

    c_ref[...] = jnp.zeros_like(c_ref)

    eye = row == col

    def local_scores(d, c):
        t0 = pl.multiple_of(c * L, L)
        q = q_ref[pl.ds(t0, L), :]
        k = k_ref[pl.ds(t0, L), :]
        v = v_ref[pl.ds(t0, L), :]
        return dict(d=d, t0=t0, q=q, k=k, v=v, s_qk=_dot_nt(q, k))

    def local_gates(ch):
        d, t0 = ch["d"], ch["t0"]
        valid = tril if d == 0 else triu
        ich = d * ML_HEADS + head
        pick = sub8 == ich
        gi = gt_ref[0:ng, pl.ds(t0, L)] + gbs_ref[0:ng, :]
        gf = _log_sigmoid(gt_ref[ng:2 * ng, pl.ds(t0, L)] + gbs_ref[ng:2 * ng, :])
        li_row = jnp.sum(jnp.where(pick, gi, 0.0), axis=0, keepdims=True)
        lf_row = jnp.sum(jnp.where(pick, gf, 0.0), axis=0, keepdims=True)
        g_tile = g_ref[pl.ds(t0, L), :] + gbl_ref[...]
        li_col = jnp.sum(jnp.where(col == ich, g_tile, 0.0), axis=1, keepdims=True)
        b_col = jnp.sum(jnp.where(valid, lf_row, 0.0), axis=1, keepdims=True)
        b_row = jnp.sum(jnp.where(eye, b_col, 0.0), axis=0, keepdims=True)
        b_last = jnp.sum(lf_row, axis=1, keepdims=True)
        dlog = jnp.where(valid, b_col - b_row + li_row, NEG_BIG)
        m_loc = jnp.max(dlog, axis=1, keepdims=True)
        a_loc = ch["s_qk"] * jnp.exp(dlog - m_loc)
        logw = b_last - b_col + li_col
        ch.update(b_col=b_col, b_last=b_last, m_loc=m_loc, a_loc=a_loc,
                  rowsum=jnp.sum(a_loc, axis=1, keepdims=True), logw=logw,
                  m_w=jnp.max(logw, axis=0, keepdims=True))

    def state_weights(ch, m_st):
        m_new = jnp.maximum(ch["b_last"] + m_st, ch["m_w"])
        kw = ch["k"].astype(F32) * jnp.exp(ch["logw"] - m_new)
        ch.update(m_in=m_st, decay=jnp.exp(ch["b_last"] + m_st - m_new), kw=kw,
                  ksum=jnp.sum(kw, axis=0, keepdims=True))
        return m_new

    def local_matmuls(ch):
        ch["h_loc"] = _dot(ch["a_loc"].astype(BF16), ch["v"])
        ch["kv"] = _dot_tn(ch["kw"].astype(BF16), ch["v"])

    def state_step(ch, n_st, h_ref):
        d = ch["d"]
        c_st = c_ref[d]
        q = ch["q"]
        qc = _dot(q, c_st.astype(BF16))
        qn = jnp.sum(q.astype(F32) * n_st, axis=1, keepdims=True)
        inter = ch["b_col"] + ch["m_in"]
        m_t = jnp.maximum(ch["m_loc"], inter)
        w_inter = jnp.exp(inter - m_t)
        w_loc = jnp.exp(ch["m_loc"] - m_t)
        den = w_inter * qn + w_loc * ch["rowsum"]
        inv = 1.0 / jnp.maximum(jnp.abs(den), jnp.exp(-m_t))
        h_ref[pl.ds(ch["t0"], L), :] = (w_inter * inv) * qc + (w_loc * inv) * ch["h_loc"]
        c_ref[d] = ch["decay"] * c_st + ch["kv"]
        return ch["decay"] * n_st + ch["ksum"]

    cpi = ML_CHUNKS_PER_STEP

    def rec_step(i, carry):
        m_f, n_f, m_b, n_b = carry
        chains = []
        for u in range(cpi):
            chains += [local_scores(0, cpi * i + u), local_scores(1, nc - 1 - cpi * i - u)]
        for ch in chains:
            local_gates(ch)
        for u in range(cpi):
            m_f = state_weights(chains[2 * u], m_f)
            m_b = state_weights(chains[2 * u + 1], m_b)
        for ch in chains:
            local_matmuls(ch)
        for u in range(cpi):
            n_f = state_step(chains[2 * u], n_f, hf_ref)
            n_b = state_step(chains[2 * u + 1], n_b, hb_ref)
        return m_f, n_f, m_b, n_b

    assert nc % cpi == 0
    m0 = jnp.full((1, 1), NEG_BIG, F32)
    n0 = jnp.zeros((1, ML_HEAD_DIM), F32)
    lax.fori_loop(0, nc // cpi, rec_step, (m0, n0, m0, n0), unroll=4)

    def out_step(c, carry):
        t0 = pl.multiple_of(c * L, L)
        hs = hf_ref[pl.ds(t0, L), :] + hb_ref[pl.ds(t0, L), :]
        mu = jnp.mean(hs, axis=-1, keepdims=True)
        hc = hs - mu
        var = jnp.mean(hc * hc, axis=-1, keepdims=True)
        hn = hc * lax.rsqrt(var + LN_EPS) * nw_ref[...]
        og = jax.nn.sigmoid(o_ref[pl.ds(t0, L), :].astype(F32))
        y_ref[pl.ds(t0, L), :] = (hn * og).astype(y_ref.dtype)
        return carry

    lax.fori_loop(0, nc, out_step, 0, unroll=4)


def _mlstm_branch(hp3, gates3, igate_b, fgate_b, norm_w):
    bsz, s, _ = hp3.shape
    assert s % ML_CHUNK == 0
    hd = ML_HEAD_DIM
    ng = 4 * ML_HEADS
    gates_t = gates3[..., :ng].transpose(0, 2, 1)
    gb = jnp.concatenate([igate_b, fgate_b]).astype(F32)
    gbl = jnp.pad(gb, (0, LANES - ng)).reshape(1, LANES)
    gbs = gb.reshape(ng, 1)
    nw = norm_w.astype(F32).reshape(1, ML_WIDTH)
    qb, kb, vb, ob = COL_ML_Q // hd, COL_ML_K // hd, COL_ML_V // hd, COL_ML_O // hd
    return pl.pallas_call(
        functools.partial(_mlstm_kernel, seq=s),
        grid=(bsz, ML_HEADS),
        in_specs=[
            pl.BlockSpec((None, s, hd), lambda b, h: (b, 0, qb + h)),
            pl.BlockSpec((None, s, hd), lambda b, h: (b, 0, kb + h)),
            pl.BlockSpec((None, s, hd), lambda b, h: (b, 0, vb + h)),
            pl.BlockSpec((None, s, hd), lambda b, h: (b, 0, ob + h)),
            pl.BlockSpec((None, s, LANES), lambda b, h: (b, 0, 0)),
            pl.BlockSpec((None, ng, s), lambda b, h: (b, 0, 0)),
            pl.BlockSpec((1, LANES), lambda b, h: (0, 0)),
            pl.BlockSpec((ng, 1), lambda b, h: (0, 0)),
            pl.BlockSpec((1, hd), lambda b, h: (0, h)),
        ],
        out_specs=pl.BlockSpec((None, s, hd), lambda b, h: (b, 0, h)),
        out_shape=jax.ShapeDtypeStruct((bsz, s, ML_WIDTH), BF16),
        scratch_shapes=[
            pltpu.VMEM((s, hd), F32),
            pltpu.VMEM((s, hd), F32),
            pltpu.VMEM((2, hd, hd), F32),
        ],
        compiler_params=pltpu.CompilerParams(
            dimension_semantics=("parallel", "parallel"), vmem_limit_bytes=VMEM_LIMIT),
        name="mlstm",
    )(hp3, hp3, hp3, hp3, gates3, gates_t, gbl, gbs, nw)


MIX_SUB_ROWS = 256


def _mix_kernel(x_ref, yna_ref, yml_ref, gna_ref, gml_ref, wna_ref, wml_ref, wo_ref, g_ref, b_ref,
                o_ref, *, alpha):
    tm = x_ref.shape[0]
    for r0 in range(0, tm, MIX_SUB_ROWS):
        r = slice(r0, r0 + MIX_SUB_ROWS)
        a = _dot(yna_ref[r, :], wna_ref[...])
        mixed = jax.nn.sigmoid(gna_ref[r, :].astype(F32)) * a
        a = _dot(yml_ref[r, :], wml_ref[...])
        mixed = mixed + jax.nn.sigmoid(gml_ref[r, :].astype(F32)) * a
        z = alpha * x_ref[r, :] + _dot(mixed.astype(BF16), wo_ref[...])
        o_ref[r, :] = _layer_norm(z, g_ref[...], b_ref[...])


def _mix(x2, yna2, yml2, hp, wna, wml, wo, g, b, alpha):
    m, d = x2.shape
    tm = min(512, m)
    assert tm % MIX_SUB_ROWS == 0
    const = dict(pipeline_mode=pl.Buffered(1))
    return pl.pallas_call(
        functools.partial(_mix_kernel, alpha=alpha),
        grid=(m // tm,),
        in_specs=[
            pl.BlockSpec((tm, d), lambda i: (i, 0)),
            pl.BlockSpec((tm, NA_WIDTH), lambda i: (i, 0)),
            pl.BlockSpec((tm, ML_WIDTH), lambda i: (i, 0)),
            pl.BlockSpec((tm, d), lambda i: (i, COL_MERGE_NA // d)),
            pl.BlockSpec((tm, d), lambda i: (i, COL_MERGE_ML // d)),
            pl.BlockSpec((NA_WIDTH, d), lambda i: (0, 0), **const),
            pl.BlockSpec((ML_WIDTH, d), lambda i: (0, 0), **const),
            pl.BlockSpec((d, d), lambda i: (0, 0), **const),
            pl.BlockSpec((1, d), lambda i: (0, 0)),
            pl.BlockSpec((1, d), lambda i: (0, 0)),
        ],
        out_specs=pl.BlockSpec((tm, d), lambda i: (i, 0)),
        out_shape=jax.ShapeDtypeStruct((m, d), F32),
        compiler_params=pltpu.CompilerParams(
            dimension_semantics=("parallel",), vmem_limit_bytes=VMEM_LIMIT),
        name="mix",
    )(x2, yna2, yml2, hp, hp, wna, wml, wo, g, b)


FFN_TF = 512
FFN_LN_ROWS = 256


def _gelu_exact(x):
    return 0.5 * x * (1.0 + lax.erf(x * np.float32(math.sqrt(0.5))))


def _ffn_kernel(x_ref, xp_ref, xn_ref, wg_ref, wv_ref, wd_ref, cw_ref, cb_ref, g_ref, b_ref,
                o_ref, xb_ref, *, alpha, seq, tm, tail):
    i = pl.program_id(0)
    j = pl.program_id(1)
    last = pl.num_programs(1) - 1
    hl = CONV_HALO

    def glu_block(width):
        cols = slice(0, width)
        ug = _dot(xb_ref[...], wg_ref[:, cols])
        uv = _dot(xb_ref[hl:hl + tm, :], wv_ref[:, cols])
        c = _conv3_rows(ug, None, cw_ref, cb_ref, cols, tm)
        return (_gelu_exact(c) * uv).astype(BF16), cols

    @pl.when(j == 0)
    def _():
        _fill_halo_tile(xb_ref, x_ref, xp_ref, xn_ref, i, tm, seq)
        h, cols = glu_block(wg_ref.shape[1])
        o_ref[...] = _dot(h, wd_ref[cols, :])

    @pl.when((j > 0) & (j < last))
    def _():
        h, cols = glu_block(wg_ref.shape[1])
        o_ref[...] += _dot(h, wd_ref[cols, :])

    @pl.when(j == last)
    def _():
        h, cols = glu_block(tail)
        for r0 in range(0, tm, FFN_LN_ROWS):
            r = slice(r0, r0 + FFN_LN_ROWS)
            z = alpha * x_ref[r, :] + (o_ref[r, :] + _dot(h[r], wd_ref[cols, :]))
            o_ref[r, :] = _layer_norm(z, g_ref[...], b_ref[...])


def _ffn(x1, wg, wv, wd, cw, cb, g, b, alpha, seq, dff):
    m, d = x1.shape
    tm = min(1024, seq)
    tf = FFN_TF
    hl = CONV_HALO
    nff = pl.cdiv(dff, tf)
    assert seq % tm == 0 and m % tm == 0 and dff % LANES == 0 and wg.shape == (d, nff * tf) and nff >= 2
    nb = tm // hl
    last = m // hl - 1
    return pl.pallas_call(
        functools.partial(_ffn_kernel, alpha=alpha, seq=seq, tm=tm, tail=dff - (nff - 1) * tf),
        grid=(m // tm, nff),
        in_specs=[
            pl.BlockSpec((tm, d), lambda i, j: (i, 0)),
            pl.BlockSpec((hl, d), lambda i, j: (jnp.maximum(i * nb - 1, 0), 0)),
            pl.BlockSpec((hl, d), lambda i, j: (jnp.minimum((i + 1) * nb, last), 0)),
            pl.BlockSpec((d, tf), lambda i, j: (0, j)),
            pl.BlockSpec((d, tf), lambda i, j: (0, j)),
            pl.BlockSpec((tf, d), lambda i, j: (j, 0)),
            pl.BlockSpec((3, tf), lambda i, j: (0, j)),
            pl.BlockSpec((1, tf), lambda i, j: (0, j)),
            pl.BlockSpec((1, d), lambda i, j: (0, 0)),
            pl.BlockSpec((1, d), lambda i, j: (0, 0)),
        ],
        out_specs=pl.BlockSpec((tm, d), lambda i, j: (i, 0)),
        out_shape=jax.ShapeDtypeStruct((m, d), F32),
        scratch_shapes=[pltpu.VMEM((tm + 2 * hl, d), BF16)],
        compiler_params=pltpu.CompilerParams(
            dimension_semantics=("parallel", "arbitrary"), vmem_limit_bytes=VMEM_LIMIT_HIGH),
        name="ffn",
    )(x1, x1, x1, wg, wv, wd, cw, cb, g, b)


def _pad_cols(a, n):
    return jnp.pad(a, ((0, 0), (0, n - a.shape[1])))


def kernel(x, w_in, na_rpb, ml_conv_w, ml_conv_b, ml_igate_b, ml_fgate_b, ml_norm_w, w_branch_na,
           w_branch_ml, w_out, ln1_g, ln1_b, ffn_w_up, ffn_conv_w, ffn_conv_b, ffn_w_down, ln2_g, ln2_b):
    bsz, s, d = x.shape
    depth = w_in.shape[0]
    alpha = float((2 * depth) ** 0.25)
    dff = ffn_conv_w.shape[-1]
    m = bsz * s
    na3 = 3 * NA_WIDTH
    gates_at = na3 + 4 * ML_WIDTH
    merge_at = gates_at + 4 * ML_HEADS
    x2 = x.reshape(m, d)
    for l in range(depth):
        w = w_in[l]
        w_gate = _pad_cols(w[:, gates_at:merge_at], LANES).astype(BF16)
        wb = w.astype(BF16)
        hp, gates = _in_proj(x2, wb[:, merge_at:], wb, w_gate,
                             ml_conv_w[l].astype(F32), ml_conv_b[l].astype(F32).reshape(1, 2 * ML_WIDTH), s)
        hp3 = hp.reshape(bsz, s, HP_WIDTH)
        y_na = _na_attention(hp3, na_rpb[l])
        y_ml = _mlstm_branch(hp3, gates.reshape(bsz, s, LANES), ml_igate_b[l], ml_fgate_b[l],
                             ml_norm_w[l])
        x1 = _mix(x2, y_na.reshape(m, NA_WIDTH), y_ml.reshape(m, ML_WIDTH), hp,
                  w_branch_na[l].astype(BF16), w_branch_ml[l].astype(BF16), w_out[l].astype(BF16),
                  ln1_g[l].reshape(1, d), ln1_b[l].reshape(1, d), alpha)
        dffp = pl.cdiv(dff, FFN_TF) * FFN_TF
        wg = _pad_cols(ffn_w_up[l][:, :dff], dffp).astype(BF16)
        wv = _pad_cols(ffn_w_up[l][:, dff:], dffp).astype(BF16)
        wd = ffn_w_down[l].astype(BF16)
        cw = _pad_cols(ffn_conv_w[l].astype(F32), dffp)
        cb = _pad_cols(ffn_conv_b[l].astype(F32).reshape(1, dff), dffp)
        x2 = _ffn(x1, wg, wv, wd, cw, cb, ln2_g[l].reshape(1, d), ln2_b[l].reshape(1, d), alpha, s, dff)
    return x2.reshape(bsz, s, d)
```

```python
import functools
import math

import numpy as np
import jax
import jax.numpy as jnp
from jax import lax
from jax.experimental import pallas as pl
from jax.experimental.pallas import tpu as pltpu

GRID_W = 64
NA_HEADS = 16
NA_HEAD_DIM = 64
NA_WIDTH = NA_HEADS * NA_HEAD_DIM
NA_WIN_ROWS = 8
NA_WIN_COLS = 16
NA_ROW_GROUP = 2
LOG2E = math.log2(math.e)
NA_Q_SCALE = NA_HEAD_DIM ** -0.5 * LOG2E
ML_HEADS = 4
ML_HEAD_DIM = 256
ML_WIDTH = ML_HEADS * ML_HEAD_DIM
ML_CHUNK = 128
ML_CHUNKS_PER_STEP = 2
LN_EPS = 1e-5
NEG_BIG = -1e30

LANES = 128
BF16_ROWS = 16
VMEM_LIMIT = 56 * 1024 * 1024
VMEM_LIMIT_HIGH = 60 * 1024 * 1024

F32 = jnp.float32
BF16 = jnp.bfloat16

COL_MERGE_NA = 0
COL_MERGE_ML = 2048
COL_NA_Q = 4096
COL_NA_K = 5120
COL_NA_V = 6144
COL_ML_V = 7168
COL_ML_O = 8192
COL_ML_Q = 9216
COL_ML_K = 10240
HP_WIDTH = 11264


def _dot(a, b):
    return jnp.dot(a, b, preferred_element_type=F32)


def _dot_nt(a, b):
    return lax.dot_general(a, b, (((1,), (1,)), ((), ())), preferred_element_type=F32)


def _dot_tn(a, b):
    return lax.dot_general(a, b, (((0,), (0,)), ((), ())), preferred_element_type=F32)


def _layer_norm(z, g, b):
    mu = jnp.mean(z, axis=-1, keepdims=True)
    zc = z - mu
    var = jnp.mean(zc * zc, axis=-1, keepdims=True)
    return zc * lax.rsqrt(var + LN_EPS) * g + b


IN_TN = 1024
CONV_HALO = BF16_ROWS
CONV_COLS = 256


def _seq_edge_flags(i, tm, seq):
    has_prev = jnp.where((i * tm) % seq != 0, 1.0, 0.0).astype(F32)
    has_next = jnp.where(((i + 1) * tm) % seq != 0, 1.0, 0.0).astype(F32)
    return has_prev, has_next


def _fill_halo_tile(xb_ref, x_ref, xp_ref, xn_ref, i, tm, seq):
    hl = CONV_HALO
    has_prev, has_next = _seq_edge_flags(i, tm, seq)
    xb_ref[0:hl, :] = (xp_ref[...] * has_prev).astype(BF16)
    xb_ref[hl:hl + tm, :] = x_ref[...].astype(BF16)
    xb_ref[hl + tm:, :] = (xn_ref[...] * has_next).astype(BF16)


def _conv3_rows(u, u_ref, cw_ref, cb_ref, cols, tm):
    hl = CONV_HALO
    if u_ref is None:
        um1 = pltpu.roll(u, 1, 0)[hl:hl + tm]
        u0 = u[hl:hl + tm]
        up1 = pltpu.roll(u, u.shape[0] - 1, 0)[hl:hl + tm]
    else:
        u_ref[...] = u
        um1 = u_ref[hl - 1:hl - 1 + tm, :]
        u0 = u_ref[hl:hl + tm, :]
        up1 = u_ref[hl + 1:hl + 1 + tm, :]
    y = um1 * cw_ref[0:1, cols]
    y = y + u0 * cw_ref[1:2, cols]
    y = y + up1 * cw_ref[2:3, cols]
    return y + cb_ref[:, cols]


def _in_proj_kernel(x_ref, xp_ref, xn_ref, wm_ref, w_ref, wg_ref, cw_ref, cb_ref, o_ref, g_ref, xb_ref,
                    ub_ref, *, seq, tm, merge_blocks, conv_blocks):
    i = pl.program_id(0)
    j = pl.program_id(1)
    hl = CONV_HALO
    q_block, k_block = conv_blocks

    is_merge = j < merge_blocks
    is_conv = (j == q_block) | (j == k_block)

    @pl.when(j == 0)
    def _():
        _fill_halo_tile(xb_ref, x_ref, xp_ref, xn_ref, i, tm, seq)
        g_ref[...] = _dot(xb_ref[hl:hl + tm, :], wg_ref[...])
        o_ref[...] = _dot(xb_ref[hl:hl + tm, :], wm_ref[...]).astype(o_ref.dtype)

    @pl.when((j > 0) & is_merge)
    def _():
        o_ref[...] = _dot(xb_ref[hl:hl + tm, :], wm_ref[...]).astype(o_ref.dtype)

    @pl.when(jnp.logical_not(is_merge | is_conv))
    def _():
        scale = jnp.where(j == COL_NA_Q // IN_TN, NA_Q_SCALE, 1.0).astype(F32)
        o_ref[...] = (_dot(xb_ref[hl:hl + tm, :], w_ref[...]) * scale).astype(o_ref.dtype)

    @pl.when(is_conv)
    def _():
        post = jnp.where(j == k_block, ML_HEAD_DIM ** -0.5, 1.0).astype(F32)
        for n, c0 in enumerate(range(0, IN_TN, CONV_COLS)):
            cols = slice(c0, c0 + CONV_COLS)
            u = _dot(xb_ref[...], w_ref[:, cols])
            y = _conv3_rows(u, ub_ref.at[n % 2], cw_ref, cb_ref, cols, tm)
            y = y * jax.nn.sigmoid(y)
            o_ref[:, cols] = (y * post).astype(o_ref.dtype)


def _in_proj(x2, w_merge, w_rest, w_gate, conv_w, conv_b, seq):
    m, d = x2.shape
    n = HP_WIDTH
    tm = min(1024, seq)
    tn = IN_TN
    hl = CONV_HALO
    assert seq % tm == 0 and m % tm == 0 and COL_ML_Q % tn == 0 and COL_ML_K == COL_ML_Q + tn
    assert w_merge.shape[1] == COL_NA_Q and COL_NA_Q % tn == 0 and w_rest.shape[1] >= n - COL_NA_Q
    q_block = COL_ML_Q // tn
    merge_blocks = COL_NA_Q // tn
    rest_blocks = (n - COL_NA_Q) // tn
    nb = tm // hl
    last = m // hl - 1
    conv_idx = lambda i, j: (0, jnp.clip(j - q_block, 0, 1))
    na_blocks = 3 * NA_WIDTH // tn
    ml_blocks = ML_WIDTH // tn
    assert (COL_ML_V, COL_ML_Q) == (COL_NA_Q + 3 * NA_WIDTH, COL_NA_Q + 3 * NA_WIDTH + 2 * ML_WIDTH)

    def rest_src(r):
        return jnp.where(r < na_blocks, r,
                         jnp.where(r < na_blocks + 2 * ml_blocks, r + 2 * ml_blocks, r - 2 * ml_blocks))
    return pl.pallas_call(
        functools.partial(_in_proj_kernel, seq=seq, tm=tm, merge_blocks=merge_blocks,
                          conv_blocks=(q_block, q_block + 1)),
        grid=(m // tm, n // tn),
        in_specs=[
            pl.BlockSpec((tm, d), lambda i, j: (i, 0)),
            pl.BlockSpec((hl, d), lambda i, j: (jnp.maximum(i * nb - 1, 0), 0)),
            pl.BlockSpec((hl, d), lambda i, j: (jnp.minimum((i + 1) * nb, last), 0)),
            pl.BlockSpec((d, tn), lambda i, j: (0, jnp.minimum(j, merge_blocks - 1))),
            pl.BlockSpec((d, tn), lambda i, j: (0, rest_src(jnp.where(j < merge_blocks, rest_blocks - 1,
                                                                      j - merge_blocks)))),
            pl.BlockSpec((d, LANES), lambda i, j: (0, 0)),
            pl.BlockSpec((3, tn), conv_idx),
            pl.BlockSpec((1, tn), conv_idx),
        ],
        out_specs=[
            pl.BlockSpec((tm, tn), lambda i, j: (i, j)),
            pl.BlockSpec((tm, LANES), lambda i, j: (i, 0)),
        ],
        out_shape=[
            jax.ShapeDtypeStruct((m, n), BF16),
            jax.ShapeDtypeStruct((m, LANES), F32),
        ],
        scratch_shapes=[pltpu.VMEM((tm + 2 * hl, d), BF16),
                        pltpu.VMEM((2, tm + 2 * hl, CONV_COLS), F32)],
        compiler_params=pltpu.CompilerParams(
            dimension_semantics=("parallel", "arbitrary"), vmem_limit_bytes=VMEM_LIMIT),
        name="in_proj",
    )(x2, x2, x2, w_merge, w_rest, w_gate, conv_w, conv_b)


def _na_kernel(q_ref, k_ref, v_ref, bias_ref, o_ref, sca_ref, scb_ref, *, rows, group):
    kh = NA_WIN_ROWS
    nkeys = kh * GRID_W
    lane = lax.broadcasted_iota(jnp.int32, (2 * GRID_W, LANES), 1)
    qcol = lax.broadcasted_iota(jnp.int32, (2 * GRID_W, LANES), 0) & (GRID_W - 1)
    kcol = lane & (GRID_W - 1)
    wstart = jnp.clip(qcol - NA_WIN_COLS // 2, 0, GRID_W - NA_WIN_COLS)
    valid = (kcol >= wstart) & (kcol < wstart + NA_WIN_COLS)
    head_a = lax.broadcasted_iota(jnp.int32, (GRID_W, LANES), 1) < NA_HEAD_DIM

    def window(r):
        rs = min(max(r - kh // 2, 0), rows - kh)
        return rs * GRID_W, rs - r + (NA_WIN_ROWS - 1)

    def score_stage(g, sc_ref):
        for u in range(group):
            r = g * group + u
            kbase, _ = window(r)
            q = q_ref[pl.ds(r * GRID_W, GRID_W), :]
            zero = jnp.zeros_like(q)
            qs = jnp.concatenate([jnp.where(head_a, q, zero), jnp.where(head_a, zero, q)], axis=0)
            sc_ref[u] = _dot_nt(qs, k_ref[pl.ds(kbase, nkeys), :])

    def value_stage(g, sc_ref):
        for u in range(group):
            r = g * group + u
            kbase, dr0 = window(r)
            s = []
            for c in range(kh // 2):
                bias = jnp.concatenate([bias_ref[0, dr0 + 2 * c], bias_ref[1, dr0 + 2 * c]], axis=0)
                sc_c = sc_ref[u, :, c * LANES:(c + 1) * LANES] + bias
                s.append(jnp.where(valid, sc_c, NEG_BIG))
            m = jnp.maximum(jnp.maximum(s[0], s[1]), jnp.maximum(s[2], s[3]))
            m = jnp.max(m, axis=-1, keepdims=True)
            p = [jnp.exp2(s_c - m) for s_c in s]
            l = jnp.sum((p[0] + p[1]) + (p[2] + p[3]), axis=-1, keepdims=True)
            pb = jnp.concatenate([p_c.astype(BF16) for p_c in p], axis=1)
            acc = _dot(pb, v_ref[pl.ds(kbase, nkeys), :]) / l
            o = jnp.where(head_a, acc[:GRID_W], acc[GRID_W:])
            o_ref[pl.ds(r * GRID_W, GRID_W), :] = o.astype(o_ref.dtype)

    ngroups = rows // group
    bufs = (sca_ref, scb_ref)
    score_stage(0, bufs[0])
    for g in range(ngroups):
        if g + 1 < ngroups:
            score_stage(g + 1, bufs[(g + 1) % 2])
        value_stage(g, bufs[g % 2])


def _na_bias_expansion():
    nrel = 2 * NA_WIN_COLS - 1
    qc = np.arange(GRID_W)[:, None]
    kc = np.arange(GRID_W)[None, :]
    dc = np.clip(kc - qc + NA_WIN_COLS - 1, 0, nrel - 1)
    oh = np.zeros((2, nrel, GRID_W, 2, GRID_W), np.float32)
    for half in range(2):
        oh[half, dc, qc, half, kc] = 1.0
    return oh.reshape(2 * nrel, GRID_W * LANES)


def _na_bias_table(rpb):
    r2 = jnp.concatenate([rpb[:, :-1], rpb[:, 1:]], axis=-1)
    r2 = r2.reshape(NA_HEADS * (2 * NA_WIN_ROWS - 2), -1)
    t = jnp.dot(r2, jnp.asarray(_na_bias_expansion()), precision=lax.Precision.HIGHEST)
    return t.reshape(NA_HEADS // 2, 2, 2 * NA_WIN_ROWS - 2, GRID_W, LANES)


def _na_attention(hp3, rpb):
    bsz, s, _ = hp3.shape
    rows = s // GRID_W
    assert s % GRID_W == 0 and rows >= NA_WIN_ROWS
    table = _na_bias_table(rpb.astype(F32) * LOG2E)
    qb, kb, vb = COL_NA_Q // LANES, COL_NA_K // LANES, COL_NA_V // LANES
    group = min(NA_ROW_GROUP, rows // 2)
    assert rows % (2 * group) == 0
    sc_buf = pltpu.VMEM((group, 2 * GRID_W, NA_WIN_ROWS * GRID_W), F32)
    return pl.pallas_call(
        functools.partial(_na_kernel, rows=rows, group=group),
        grid=(bsz, NA_HEADS // 2),
        in_specs=[
            pl.BlockSpec((None, s, LANES), lambda b, h: (b, 0, qb + h)),
            pl.BlockSpec((None, s, LANES), lambda b, h: (b, 0, kb + h)),
            pl.BlockSpec((None, s, LANES), lambda b, h: (b, 0, vb + h)),
            pl.BlockSpec((None, 2, 2 * NA_WIN_ROWS - 2, GRID_W, LANES), lambda b, h: (h, 0, 0, 0, 0)),
        ],
        out_specs=pl.BlockSpec((None, s, LANES), lambda b, h: (b, 0, h)),
        out_shape=jax.ShapeDtypeStruct((bsz, s, NA_WIDTH), BF16),
        scratch_shapes=[sc_buf, sc_buf],
        compiler_params=pltpu.CompilerParams(
            dimension_semantics=("parallel", "parallel"), vmem_limit_bytes=VMEM_LIMIT),
        name="na_attn",
    )(hp3, hp3, hp3, table)


def _log_sigmoid(x):
    return jnp.minimum(x, 0.0) - jnp.log1p(jnp.exp(-jnp.abs(x)))


def _mlstm_kernel(q_ref, k_ref, v_ref, o_ref, g_ref, gt_ref, gbl_ref, gbs_ref, nw_ref, y_ref,
                  hf_ref, hb_ref, c_ref, *, seq):
    L = ML_CHUNK
    nc = seq // L
    head = pl.program_id(1)
    row = lax.broadcasted_iota(jnp.int32, (L, L), 0)
    col = lax.broadcasted_iota(jnp.int32, (L, L), 1)
    tril = row >= col
    triu = row <= col
    ng = 2 * ML_HEADS
    sub8 = lax.broadcasted_iota(jnp.int32, (ng, L), 0)

    c_ref[...] = jnp.zeros_like(c_ref)

    eye = row == col

    def local_scores(d, c):
        t0 = pl.multiple_of(c * L, L)
        q = q_ref[pl.ds(t0, L), :]
        k = k_ref[pl.ds(t0, L), :]
        v = v_ref[pl.ds(t0, L), :]
        return dict(d=d, t0=t0, q=q, k=k, v=v, s_qk=_dot_nt(q, k))

    def local_gates(ch):
        d, t0 = ch["d"], ch["t0"]
        valid = tril if d == 0 else triu
        ich = d * ML_HEADS + head
        pick = sub8 == ich
        gi = gt_ref[0:ng, pl.ds(t0, L)] + gbs_ref[0:ng, :]
        gf = _log_sigmoid(gt_ref[ng:2 * ng, pl.ds(t0, L)] + gbs_ref[ng:2 * ng, :])
        li_row = jnp.sum(jnp.where(pick, gi, 0.0), axis=0, keepdims=True)
        lf_row = jnp.sum(jnp.where(pick, gf, 0.0), axis=0, keepdims=True)
        g_tile = g_ref[pl.ds(t0, L), :] + gbl_ref[...]
        li_col = jnp.sum(jnp.where(col == ich, g_tile, 0.0), axis=1, keepdims=True)
        b_col = jnp.sum(jnp.where(valid, lf_row, 0.0), axis=1, keepdims=True)
        b_row = jnp.sum(jnp.where(eye, b_col, 0.0), axis=0, keepdims=True)
        b_last = jnp.sum(lf_row, axis=1, keepdims=True)
        dlog = jnp.where(valid, b_col - b_row + li_row, NEG_BIG)
        m_loc = jnp.max(dlog, axis=1, keepdims=True)
        a_loc = ch["s_qk"] * jnp.exp(dlog - m_loc)
        logw = b_last - b_col + li_col
        ch.update(b_col=b_col, b_last=b_last, m_loc=m_loc, a_loc=a_loc,
                  rowsum=jnp.sum(a_loc, axis=1, keepdims=True), logw=logw,
                  m_w=jnp.max(logw, axis=0, keepdims=True))

    def state_weights(ch, m_st):
        m_new = jnp.maximum(ch["b_last"] + m_st, ch["m_w"])
        kw = ch["k"].astype(F32) * jnp.exp(ch["logw"] - m_new)
        ch.update(m_in=m_st, decay=jnp.exp(ch["b_last"] + m_st - m_new), kw=kw,
                  ksum=jnp.sum(kw, axis=0, keepdims=True))
        return m_new

    def local_matmuls(ch):
        ch["h_loc"] = _dot(ch["a_loc"].astype(BF16), ch["v"])
        ch["kv"] = _dot_tn(ch["kw"].astype(BF16), ch["v"])

    def state_step(ch, n_st, h_ref):
        d = ch["d"]
        c_st = c_ref[d]
        q = ch["q"]
        qc = _dot(q, c_st.astype(BF16))
        qn = jnp.sum(q.astype(F32) * n_st, axis=1, keepdims=True)
        inter = ch["b_col"] + ch["m_in"]
        m_t = jnp.maximum(ch["m_loc"], inter)
        w_inter = jnp.exp(inter - m_t)
        w_loc = jnp.exp(ch["m_loc"] - m_t)
        den = w_inter * qn + w_loc * ch["rowsum"]
        inv = 1.0 / jnp.maximum(jnp.abs(den), jnp.exp(-m_t))
        h_ref[pl.ds(ch["t0"], L), :] = (w_inter * inv) * qc + (w_loc * inv) * ch["h_loc"]
        c_ref[d] = ch["decay"] * c_st + ch["kv"]
        return ch["decay"] * n_st + ch["ksum"]

    cpi = ML_CHUNKS_PER_STEP

    def rec_step(i, carry):
        m_f, n_f, m_b, n_b = carry
        chains = []
        for u in range(cpi):
            chains += [local_scores(0, cpi * i + u), local_scores(1, nc - 1 - cpi * i - u)]
        for ch in chains:
            local_gates(ch)
        for u in range(cpi):
            m_f = state_weights(chains[2 * u], m_f)
            m_b = state_weights(chains[2 * u + 1], m_b)
        for ch in chains:
            local_matmuls(ch)
        for u in range(cpi):
            n_f = state_step(chains[2 * u], n_f, hf_ref)
            n_b = state_step(chains[2 * u + 1], n_b, hb_ref)
        return m_f, n_f, m_b, n_b

    assert nc % cpi == 0
    m0 = jnp.full((1, 1), NEG_BIG, F32)
    n0 = jnp.zeros((1, ML_HEAD_DIM), F32)
    lax.fori_loop(0, nc // cpi, rec_step, (m0, n0, m0, n0), unroll=4)

    def out_step(c, carry):
        t0 = pl.multiple_of(c * L, L)
        hs = hf_ref[pl.ds(t0, L), :] + hb_ref[pl.ds(t0, L), :]
        mu = jnp.mean(hs, axis=-1, keepdims=True)
        hc = hs - mu
        var = jnp.mean(hc * hc, axis=-1, keepdims=True)
        hn = hc * lax.rsqrt(var + LN_EPS) * nw_ref[...]
        og = jax.nn.sigmoid(o_ref[pl.ds(t0, L), :].astype(F32))
        y_ref[pl.ds(t0, L), :] = (hn * og).astype(y_ref.dtype)
        return carry

    lax.fori_loop(0, nc, out_step, 0, unroll=4)


def _mlstm_branch(hp3, gates3, igate_b, fgate_b, norm_w):
    bsz, s, _ = hp3.shape
    assert s % ML_CHUNK == 0
    hd = ML_HEAD_DIM
    ng = 4 * ML_HEADS
    gates_t = gates3[..., :ng].transpose(0, 2, 1)
    gb = jnp.concatenate([igate_b, fgate_b]).astype(F32)
    gbl = jnp.pad(gb, (0, LANES - ng)).reshape(1, LANES)
    gbs = gb.reshape(ng, 1)
    nw = norm_w.astype(F32).reshape(1, ML_WIDTH)
    qb, kb, vb, ob = COL_ML_Q // hd, COL_ML_K // hd, COL_ML_V // hd, COL_ML_O // hd
    return pl.pallas_call(
        functools.partial(_mlstm_kernel, seq=s),
        grid=(bsz, ML_HEADS),
        in_specs=[
            pl.BlockSpec((None, s, hd), lambda b, h: (b, 0, qb + h)),
            pl.BlockSpec((None, s, hd), lambda b, h: (b, 0, kb + h)),
            pl.BlockSpec((None, s, hd), lambda b, h: (b, 0, vb + h)),
            pl.BlockSpec((None, s, hd), lambda b, h: (b, 0, ob + h)),
            pl.BlockSpec((None, s, LANES), lambda b, h: (b, 0, 0)),
            pl.BlockSpec((None, ng, s), lambda b, h: (b, 0, 0)),
            pl.BlockSpec((1, LANES), lambda b, h: (0, 0)),
            pl.BlockSpec((ng, 1), lambda b, h: (0, 0)),
            pl.BlockSpec((1, hd), lambda b, h: (0, h)),
        ],
        out_specs=pl.BlockSpec((None, s, hd), lambda b, h: (b, 0, h)),
        out_shape=jax.ShapeDtypeStruct((bsz, s, ML_WIDTH), BF16),
        scratch_shapes=[
            pltpu.VMEM((s, hd), F32),
            pltpu.VMEM((s, hd), F32),
            pltpu.VMEM((2, hd, hd), F32),
        ],
        compiler_params=pltpu.CompilerParams(
            dimension_semantics=("parallel", "parallel"), vmem_limit_bytes=VMEM_LIMIT),
        name="mlstm",
    )(hp3, hp3, hp3, hp3, gates3, gates_t, gbl, gbs, nw)


MIX_SUB_ROWS = 256


def _mix_kernel(x_ref, yna_ref, yml_ref, gna_ref, gml_ref, wna_ref, wml_ref, wo_ref, g_ref, b_ref,
                o_ref, *, alpha):
    tm = x_ref.shape[0]
    for r0 in range(0, tm, MIX_SUB_ROWS):
        r = slice(r0, r0 + MIX_SUB_ROWS)
        a = _dot(yna_ref[r, :], wna_ref[...])
        mixed = jax.nn.sigmoid(gna_ref[r, :].astype(F32)) * a
        a = _dot(yml_ref[r, :], wml_ref[...])
        mixed = mixed + jax.nn.sigmoid(gml_ref[r, :].astype(F32)) * a
        z = alpha * x_ref[r, :] + _dot(mixed.astype(BF16), wo_ref[...])
        o_ref[r, :] = _layer_norm(z, g_ref[...], b_ref[...])


def _mix(x2, yna2, yml2, hp, wna, wml, wo, g, b, alpha):
    m, d = x2.shape
    tm = min(512, m)
    assert tm % MIX_SUB_ROWS == 0
    const = dict(pipeline_mode=pl.Buffered(1))
    return pl.pallas_call(
        functools.partial(_mix_kernel, alpha=alpha),
        grid=(m // tm,),
        in_specs=[
            pl.BlockSpec((tm, d), lambda i: (i, 0)),
            pl.BlockSpec((tm, NA_WIDTH), lambda i: (i, 0)),
            pl.BlockSpec((tm, ML_WIDTH), lambda i: (i, 0)),
            pl.BlockSpec((tm, d), lambda i: (i, COL_MERGE_NA // d)),
            pl.BlockSpec((tm, d), lambda i: (i, COL_MERGE_ML // d)),
            pl.BlockSpec((NA_WIDTH, d), lambda i: (0, 0), **const),
            pl.BlockSpec((ML_WIDTH, d), lambda i: (0, 0), **const),
            pl.BlockSpec((d, d), lambda i: (0, 0), **const),
            pl.BlockSpec((1, d), lambda i: (0, 0)),
            pl.BlockSpec((1, d), lambda i: (0, 0)),
        ],
        out_specs=pl.BlockSpec((tm, d), lambda i: (i, 0)),
        out_shape=jax.ShapeDtypeStruct((m, d), F32),
        compiler_params=pltpu.CompilerParams(
            dimension_semantics=("parallel",), vmem_limit_bytes=VMEM_LIMIT),
        name="mix",
    )(x2, yna2, yml2, hp, hp, wna, wml, wo, g, b)


FFN_TF = 512
FFN_LN_ROWS = 256


def _gelu_exact(x):
    return 0.5 * x * (1.0 + lax.erf(x * np.float32(math.sqrt(0.5))))


def _ffn_kernel(x_ref, xp_ref, xn_ref, wg_ref, wv_ref, wd_ref, cw_ref, cb_ref, g_ref, b_ref,
                o_ref, xb_ref, *, alpha, seq, tm, tail):
    i = pl.program_id(0)
    j = pl.program_id(1)
    last = pl.num_programs(1) - 1
    hl = CONV_HALO

    def glu_block(width):
        cols = slice(0, width)
        ug = _dot(xb_ref[...], wg_ref[:, cols])
        uv = _dot(xb_ref[hl:hl + tm, :], wv_ref[:, cols])
        c = _conv3_rows(ug, None, cw_ref, cb_ref, cols, tm)
        return (_gelu_exact(c) * uv).astype(BF16), cols

    @pl.when(j == 0)
    def _():
        _fill_halo_tile(xb_ref, x_ref, xp_ref, xn_ref, i, tm, seq)
        h, cols = glu_block(wg_ref.shape[1])
        o_ref[...] = _dot(h, wd_ref[cols, :])

    @pl.when((j > 0) & (j < last))
    def _():
        h, cols = glu_block(wg_ref.shape[1])
        o_ref[...] += _dot(h, wd_ref[cols, :])

    @pl.when(j == last)
    def _():
        h, cols = glu_block(tail)
        for r0 in range(0, tm, FFN_LN_ROWS):
            r = slice(r0, r0 + FFN_LN_ROWS)
            z = alpha * x_ref[r, :] + (o_ref[r, :] + _dot(h[r], wd_ref[cols, :]))
            o_ref[r, :] = _layer_norm(z, g_ref[...], b_ref[...])


def _ffn(x1, wg, wv, wd, cw, cb, g, b, alpha, seq, dff):
    m, d = x1.shape
    tm = min(1024, seq)
    tf = FFN_TF
    hl = CONV_HALO
    nff = pl.cdiv(dff, tf)
    assert seq % tm == 0 and m % tm == 0 and dff % LANES == 0 and wg.shape == (d, nff * tf) and nff >= 2
    nb = tm // hl
    last = m // hl - 1
    return pl.pallas_call(
        functools.partial(_ffn_kernel, alpha=alpha, seq=seq, tm=tm, tail=dff - (nff - 1) * tf),
        grid=(m // tm, nff),
        in_specs=[
            pl.BlockSpec((tm, d), lambda i, j: (i, 0)),
            pl.BlockSpec((hl, d), lambda i, j: (jnp.maximum(i * nb - 1, 0), 0)),
            pl.BlockSpec((hl, d), lambda i, j: (jnp.minimum((i + 1) * nb, last), 0)),
            pl.BlockSpec((d, tf), lambda i, j: (0, j)),
            pl.BlockSpec((d, tf), lambda i, j: (0, j)),
            pl.BlockSpec((tf, d), lambda i, j: (j, 0)),
            pl.BlockSpec((3, tf), lambda i, j: (0, j)),
            pl.BlockSpec((1, tf), lambda i, j: (0, j)),
            pl.BlockSpec((1, d), lambda i, j: (0, 0)),
            pl.BlockSpec((1, d), lambda i, j: (0, 0)),
        ],
        out_specs=pl.BlockSpec((tm, d), lambda i, j: (i, 0)),
        out_shape=jax.ShapeDtypeStruct((m, d), F32),
        scratch_shapes=[pltpu.VMEM((tm + 2 * hl, d), BF16)],
        compiler_params=pltpu.CompilerParams(
            dimension_semantics=("parallel", "arbitrary"), vmem_limit_bytes=VMEM_LIMIT_HIGH),
        name="ffn",
    )(x1, x1, x1, wg, wv, wd, cw, cb, g, b)


def _pad_cols(a, n):
    return jnp.pad(a, ((0, 0), (0, n - a.shape[1])))


def kernel(x, w_in, na_rpb, ml_conv_w, ml_conv_b, ml_igate_b, ml_fgate_b, ml_norm_w, w_branch_na,
           w_branch_ml, w_out, ln1_g, ln1_b, ffn_w_up, ffn_conv_w, ffn_conv_b, ffn_w_down, ln2_g, ln2_b):
    bsz, s, d = x.shape
    depth = w_in.shape[0]
    alpha = float((2 * depth) ** 0.25)
    dff = ffn_conv_w.shape[-1]
    m = bsz * s
    na3 = 3 * NA_WIDTH
    gates_at = na3 + 4 * ML_WIDTH
    merge_at = gates_at + 4 * ML_HEADS
    x2 = x.reshape(m, d)
    for l in range(depth):
        w = w_in[l]
        w_gate = _pad_cols(w[:, gates_at:merge_at], LANES).astype(BF16)
        wb = w.astype(BF16)
        hp, gates = _in_proj(x2, wb[:, merge_at:], wb, w_gate,
                             ml_conv_w[l].astype(F32), ml_conv_b[l].astype(F32).reshape(1, 2 * ML_WIDTH), s)
        hp3 = hp.reshape(bsz, s, HP_WIDTH)
        y_na = _na_attention(hp3, na_rpb[l])
        y_ml = _mlstm_branch(hp3, gates.reshape(bsz, s, LANES), ml_igate_b[l], ml_fgate_b[l],
                             ml_norm_w[l])
        x1 = _mix(x2, y_na.reshape(m, NA_WIDTH), y_ml.reshape(m, ML_WIDTH), hp,
                  w_branch_na[l].astype(BF16), w_branch_ml[l].astype(BF16), w_out[l].astype(BF16),
                  ln1_g[l].reshape(1, d), ln1_b[l].reshape(1, d), alpha)
        dffp = pl.cdiv(dff, FFN_TF) * FFN_TF
        wg = _pad_cols(ffn_w_up[l][:, :dff], dffp).astype(BF16)
        wv = _pad_cols(ffn_w_up[l][:, dff:], dffp).astype(BF16)
        wd = ffn_w_down[l].astype(BF16)
        cw = _pad_cols(ffn_conv_w[l].astype(F32), dffp)
        cb = _pad_cols(ffn_conv_b[l].astype(F32).reshape(1, dff), dffp)
        x2 = _ffn(x1, wg, wv, wd, cw, cb, ln2_g[l].reshape(1, d), ln2_b[l].reshape(1, d), alpha, s, dff)
    return x2.reshape(bsz, s, d)
```

```python
import functools
import math

import numpy as np
import jax
import jax.numpy as jnp
from jax import lax
from jax.experimental import pallas as pl
from jax.experimental.pallas import tpu as pltpu

GRID_W = 64
NA_HEADS = 16
NA_HEAD_DIM = 64
NA_WIDTH = NA_HEADS * NA_HEAD_DIM
NA_WIN_ROWS = 8
NA_WIN_COLS = 16
NA_ROW_GROUP = 2
LOG2E = math.log2(math.e)
NA_Q_SCALE = NA_HEAD_DIM ** -0.5 * LOG2E
ML_HEADS = 4
ML_HEAD_DIM = 256
ML_WIDTH = ML_HEADS * ML_HEAD_DIM
ML_CHUNK = 128
ML_CHUNKS_PER_STEP = 2
LN_EPS = 1e-5
NEG_BIG = -1e30

LANES = 128
BF16_ROWS = 16
VMEM_LIMIT = 56 * 1024 * 1024
VMEM_LIMIT_HIGH = 60 * 1024 * 1024

IN_TM = 1024
MIX_TM = 512
FFN_TM = 1024

F32 = jnp.float32
BF16 = jnp.bfloat16

COL_MERGE_NA = 0
COL_MERGE_ML = 2048
COL_NA_Q = 4096
COL_NA_K = 5120
COL_NA_V = 6144
COL_ML_V = 7168
COL_ML_O = 8192
COL_ML_Q = 9216
COL_ML_K = 10240
HP_WIDTH = 11264


def _dot(a, b):
    return jnp.dot(a, b, preferred_element_type=F32)


def _dot_nt(a, b):
    return lax.dot_general(a, b, (((1,), (1,)), ((), ())), preferred_element_type=F32)


def _dot_tn(a, b):
    return lax.dot_general(a, b, (((0,), (0,)), ((), ())), preferred_element_type=F32)


def _layer_norm(z, g, b):
    mu = jnp.mean(z, axis=-1, keepdims=True)
    zc = z - mu
    var = jnp.mean(zc * zc, axis=-1, keepdims=True)
    return zc * lax.rsqrt(var + LN_EPS) * g + b


IN_TN = 1024
CONV_HALO = BF16_ROWS
CONV_COLS = 256


def _seq_edge_flags(i, tm, seq):
    has_prev = jnp.where((i * tm) % seq != 0, 1.0, 0.0).astype(F32)
    has_next = jnp.where(((i + 1) * tm) % seq != 0, 1.0, 0.0).astype(F32)
    return has_prev, has_next


def _fill_halo_tile(xb_ref, x_ref, xp_ref, xn_ref, i, tm, seq):
    hl = CONV_HALO
    has_prev, has_next = _seq_edge_flags(i, tm, seq)
    xb_ref[0:hl, :] = (xp_ref[...] * has_prev).astype(BF16)
    xb_ref[hl:hl + tm, :] = x_ref[...].astype(BF16)
    xb_ref[hl + tm:, :] = (xn_ref[...] * has_next).astype(BF16)


def _conv3_rows(u, u_ref, cw_ref, cb_ref, cols, tm):
    hl = CONV_HALO
    if u_ref is None:
        um1 = pltpu.roll(u, 1, 0)[hl:hl + tm]
        u0 = u[hl:hl + tm]
        up1 = pltpu.roll(u, u.shape[0] - 1, 0)[hl:hl + tm]
    else:
        u_ref[...] = u
        um1 = u_ref[hl - 1:hl - 1 + tm, :]
        u0 = u_ref[hl:hl + tm, :]
        up1 = u_ref[hl + 1:hl + 1 + tm, :]
    y = um1 * cw_ref[0:1, cols]
    y = y + u0 * cw_ref[1:2, cols]
    y = y + up1 * cw_ref[2:3, cols]
    return y + cb_ref[:, cols]


def _in_proj_kernel(x_ref, xp_ref, xn_ref, wm_ref, w_ref, wg_ref, cw_ref, cb_ref, o_ref, g_ref, xb_ref,
                    ub_ref, *, seq, tm, merge_blocks, conv_blocks):
    i = pl.program_id(0)
    j = pl.program_id(1)
    hl = CONV_HALO
    q_block, k_block = conv_blocks

    is_merge = j < merge_blocks
    is_conv = (j == q_block) | (j == k_block)

    @pl.when(j == 0)
    def _():
        _fill_halo_tile(xb_ref, x_ref, xp_ref, xn_ref, i, tm, seq)
        g_ref[...] = _dot(xb_ref[hl:hl + tm, :], wg_ref[...])
        o_ref[...] = _dot(xb_ref[hl:hl + tm, :], wm_ref[...]).astype(o_ref.dtype)

    @pl.when((j > 0) & is_merge)
    def _():
        o_ref[...] = _dot(xb_ref[hl:hl + tm, :], wm_ref[...]).astype(o_ref.dtype)

    @pl.when(jnp.logical_not(is_merge | is_conv))
    def _():
        scale = jnp.where(j == COL_NA_Q // IN_TN, NA_Q_SCALE, 1.0).astype(F32)
        o_ref[...] = (_dot(xb_ref[hl:hl + tm, :], w_ref[...]) * scale).astype(o_ref.dtype)

    @pl.when(is_conv)
    def _():
        post = jnp.where(j == k_block, ML_HEAD_DIM ** -0.5, 1.0).astype(F32)
        for n, c0 in enumerate(range(0, IN_TN, CONV_COLS)):
            cols = slice(c0, c0 + CONV_COLS)
            u = _dot(xb_ref[...], w_ref[:, cols])
            y = _conv3_rows(u, ub_ref.at[n % 2], cw_ref, cb_ref, cols, tm)
            y = y * jax.nn.sigmoid(y)
            o_ref[:, cols] = (y * post).astype(o_ref.dtype)


def _in_proj(x2, w_merge, w_rest, w_gate, conv_w, conv_b, seq):
    m, d = x2.shape
    n = HP_WIDTH
    tm = min(IN_TM, seq)
    tn = IN_TN
    hl = CONV_HALO
    assert seq % tm == 0 and m % tm == 0 and COL_ML_Q % tn == 0 and COL_ML_K == COL_ML_Q + tn
    assert w_merge.shape[1] == COL_NA_Q and COL_NA_Q % tn == 0 and w_rest.shape[1] >= n - COL_NA_Q
    q_block = COL_ML_Q // tn
    merge_blocks = COL_NA_Q // tn
    rest_blocks = (n - COL_NA_Q) // tn
    nb = tm // hl
    last = m // hl - 1
    conv_idx = lambda i, j: (0, jnp.clip(j - q_block, 0, 1))
    na_blocks = 3 * NA_WIDTH // tn
    ml_blocks = ML_WIDTH // tn
    assert (COL_ML_V, COL_ML_Q) == (COL_NA_Q + 3 * NA_WIDTH, COL_NA_Q + 3 * NA_WIDTH + 2 * ML_WIDTH)

    def rest_src(r):
        return jnp.where(r < na_blocks, r,
                         jnp.where(r < na_blocks + 2 * ml_blocks, r + 2 * ml_blocks, r - 2 * ml_blocks))
    return pl.pallas_call(
        functools.partial(_in_proj_kernel, seq=seq, tm=tm, merge_blocks=merge_blocks,
                          conv_blocks=(q_block, q_block + 1)),
        grid=(m // tm, n // tn),
        in_specs=[
            pl.BlockSpec((tm, d), lambda i, j: (i, 0)),
            pl.BlockSpec((hl, d), lambda i, j: (jnp.maximum(i * nb - 1, 0), 0)),
            pl.BlockSpec((hl, d), lambda i, j: (jnp.minimum((i + 1) * nb, last), 0)),
            pl.BlockSpec((d, tn), lambda i, j: (0, jnp.minimum(j, merge_blocks - 1))),
            pl.BlockSpec((d, tn), lambda i, j: (0, rest_src(jnp.where(j < merge_blocks, rest_blocks - 1,
                                                                      j - merge_blocks)))),
            pl.BlockSpec((d, LANES), lambda i, j: (0, 0)),
            pl.BlockSpec((3, tn), conv_idx),
            pl.BlockSpec((1, tn), conv_idx),
        ],
        out_specs=[
            pl.BlockSpec((tm, tn), lambda i, j: (i, j)),
            pl.BlockSpec((tm, LANES), lambda i, j: (i, 0)),
        ],
        out_shape=[
            jax.ShapeDtypeStruct((m, n), BF16),
            jax.ShapeDtypeStruct((m, LANES), F32),
        ],
        scratch_shapes=[pltpu.VMEM((tm + 2 * hl, d), BF16),
                        pltpu.VMEM((2, tm + 2 * hl, CONV_COLS), F32)],
        compiler_params=pltpu.CompilerParams(
            dimension_semantics=("parallel", "arbitrary"), vmem_limit_bytes=VMEM_LIMIT),
        name="in_proj",
    )(x2, x2, x2, w_merge, w_rest, w_gate, conv_w, conv_b)


def _na_kernel(q_ref, k_ref, v_ref, bias_ref, o_ref, sca_ref, scb_ref, *, rows, group):
    kh = NA_WIN_ROWS
    nkeys = kh * GRID_W
    lane = lax.broadcasted_iota(jnp.int32, (2 * GRID_W, LANES), 1)
    qcol = lax.broadcasted_iota(jnp.int32, (2 * GRID_W, LANES), 0) & (GRID_W - 1)
    kcol = lane & (GRID_W - 1)
    wstart = jnp.clip(qcol - NA_WIN_COLS // 2, 0, GRID_W - NA_WIN_COLS)
    valid = (kcol >= wstart) & (kcol < wstart + NA_WIN_COLS)
    head_a = lax.broadcasted_iota(jnp.int32, (GRID_W, LANES), 1) < NA_HEAD_DIM

    def window(r):
        rs = min(max(r - kh // 2, 0), rows - kh)
        return rs * GRID_W, rs - r + (NA_WIN_ROWS - 1)

    def score_stage(g, sc_ref):
        for u in range(group):
            r = g * group + u
            kbase, _ = window(r)
            q = q_ref[pl.ds(r * GRID_W, GRID_W), :]
            zero = jnp.zeros_like(q)
            qs = jnp.concatenate([jnp.where(head_a, q, zero), jnp.where(head_a, zero, q)], axis=0)
            sc_ref[u] = _dot_nt(qs, k_ref[pl.ds(kbase, nkeys), :])

    def value_stage(g, sc_ref):
        for u in range(group):
            r = g * group + u
            kbase, dr0 = window(r)
            s = []
            for c in range(kh // 2):
                bias = jnp.concatenate([bias_ref[0, dr0 + 2 * c], bias_ref[1, dr0 + 2 * c]], axis=0)
                sc_c = sc_ref[u, :, c * LANES:(c + 1) * LANES] + bias
                s.append(jnp.where(valid, sc_c, NEG_BIG))
            m = jnp.maximum(jnp.maximum(s[0], s[1]), jnp.maximum(s[2], s[3]))
            m = jnp.max(m, axis=-1, keepdims=True)
            p = [jnp.exp2(s_c - m) for s_c in s]
            l = jnp.sum((p[0] + p[1]) + (p[2] + p[3]), axis=-1, keepdims=True)
            pb = jnp.concatenate([p_c.astype(BF16) for p_c in p], axis=1)
            acc = _dot(pb, v_ref[pl.ds(kbase, nkeys), :]) / l
            o = jnp.where(head_a, acc[:GRID_W], acc[GRID_W:])
            o_ref[pl.ds(r * GRID_W, GRID_W), :] = o.astype(o_ref.dtype)

    ngroups = rows // group
    bufs = (sca_ref, scb_ref)
    score_stage(0, bufs[0])
    for g in range(ngroups):
        if g + 1 < ngroups:
            score_stage(g + 1, bufs[(g + 1) % 2])
        value_stage(g, bufs[g % 2])


def _na_bias_expansion():
    nrel = 2 * NA_WIN_COLS - 1
    qc = np.arange(GRID_W)[:, None]
    kc = np.arange(GRID_W)[None, :]
    dc = np.clip(kc - qc + NA_WIN_COLS - 1, 0, nrel - 1)
    oh = np.zeros((2, nrel, GRID_W, 2, GRID_W), np.float32)
    for half in range(2):
        oh[half, dc, qc, half, kc] = 1.0
    return oh.reshape(2 * nrel, GRID_W * LANES)


def _na_bias_table(rpb):
    r2 = jnp.concatenate([rpb[:, :-1], rpb[:, 1:]], axis=-1)
    r2 = r2.reshape(NA_HEADS * (2 * NA_WIN_ROWS - 2), -1)
    t = jnp.dot(r2, jnp.asarray(_na_bias_expansion()), precision=lax.Precision.HIGHEST)
    return t.reshape(NA_HEADS // 2, 2, 2 * NA_WIN_ROWS - 2, GRID_W, LANES)


def _na_attention(hp3, rpb):
    bsz, s, _ = hp3.shape
    rows = s // GRID_W
    assert s % GRID_W == 0 and rows >= NA_WIN_ROWS
    table = _na_bias_table(rpb.astype(F32) * LOG2E)
    qb, kb, vb = COL_NA_Q // LANES, COL_NA_K // LANES, COL_NA_V // LANES
    group = min(NA_ROW_GROUP, rows // 2)
    assert rows % (2 * group) == 0
    sc_buf = pltpu.VMEM((group, 2 * GRID_W, NA_WIN_ROWS * GRID_W), F32)
    return pl.pallas_call(
        functools.partial(_na_kernel, rows=rows, group=group),
        grid=(bsz, NA_HEADS // 2),
        in_specs=[
            pl.BlockSpec((None, s, LANES), lambda b, h: (b, 0, qb + h)),
            pl.BlockSpec((None, s, LANES), lambda b, h: (b, 0, kb + h)),
            pl.BlockSpec((None, s, LANES), lambda b, h: (b, 0, vb + h)),
            pl.BlockSpec((None, 2, 2 * NA_WIN_ROWS - 2, GRID_W, LANES), lambda b, h: (h, 0, 0, 0, 0)),
        ],
        out_specs=pl.BlockSpec((None, s, LANES), lambda b, h: (b, 0, h)),
        out_shape=jax.ShapeDtypeStruct((bsz, s, NA_WIDTH), BF16),
        scratch_shapes=[sc_buf, sc_buf],
        compiler_params=pltpu.CompilerParams(
            dimension_semantics=("parallel", "parallel"), vmem_limit_bytes=VMEM_LIMIT),
        name="na_attn",
    )(hp3, hp3, hp3, table)


def _log_sigmoid(x):
    return jnp.minimum(x, 0.0) - jnp.log1p(jnp.exp(-jnp.abs(x)))


def _mlstm_kernel(q_ref, k_ref, v_ref, o_ref, g_ref, gt_ref, gbl_ref, gbs_ref, nw_ref, y_ref,
                  hf_ref, hb_ref, c_ref, *, seq):
    L = ML_CHUNK
    nc = seq // L
    head = pl.program_id(1)
    row = lax.broadcasted_iota(jnp.int32, (L, L), 0)
    col = lax.broadcasted_iota(jnp.int32, (L, L), 1)
    tril = row >= col
    triu = row <= col
    ng = 2 * ML_HEADS
    sub8 = lax.broadcasted_iota(jnp.int32, (ng, L), 0)

    c_ref[...] = jnp.zeros_like(c_ref)

    eye = row == col

    def local_scores(d, c):
        t0 = pl.multiple_of(c * L, L)
        q = q_ref[pl.ds(t0, L), :]
        k = k_ref[pl.ds(t0, L), :]
        v = v_ref[pl.ds(t0, L), :]
        return dict(d=d, t0=t0, q=q, k=k, v=v, s_qk=_dot_nt(q, k))

    def local_gates(ch):
        d, t0 = ch["d"], ch["t0"]
        valid = tril if d == 0 else triu
        ich = d * ML_HEADS + head
        pick = sub8 == ich
        gi = gt_ref[0:ng, pl.ds(t0, L)] + gbs_ref[0:ng, :]
        gf = _log_sigmoid(gt_ref[ng:2 * ng, pl.ds(t0, L)] + gbs_ref[ng:2 * ng, :])
        li_row = jnp.sum(jnp.where(pick, gi, 0.0), axis=0, keepdims=True)
        lf_row = jnp.sum(jnp.where(pick, gf, 0.0), axis=0, keepdims=True)
        g_tile = g_ref[pl.ds(t0, L), :] + gbl_ref[...]
        li_col = jnp.sum(jnp.where(col == ich, g_tile, 0.0), axis=1, keepdims=True)
        b_col = jnp.sum(jnp.where(valid, lf_row, 0.0), axis=1, keepdims=True)
        b_row = jnp.sum(jnp.where(eye, b_col, 0.0), axis=0, keepdims=True)
        b_last = jnp.sum(lf_row, axis=1, keepdims=True)
        dlog = jnp.where(valid, b_col - b_row + li_row, NEG_BIG)
        m_loc = jnp.max(dlog, axis=1, keepdims=True)
        a_loc = ch["s_qk"] * jnp.exp(dlog - m_loc)
        logw = b_last - b_col + li_col
        ch.update(b_col=b_col, b_last=b_last, m_loc=m_loc, a_loc=a_loc,
                  rowsum=jnp.sum(a_loc, axis=1, keepdims=True), logw=logw,
                  m_w=jnp.max(logw, axis=0, keepdims=True))

    def state_weights(ch, m_st):
        m_new = jnp.maximum(ch["b_last"] + m_st, ch["m_w"])
        kw = ch["k"].astype(F32) * jnp.exp(ch["logw"] - m_new)
        ch.update(m_in=m_st, decay=jnp.exp(ch["b_last"] + m_st - m_new), kw=kw,
                  ksum=jnp.sum(kw, axis=0, keepdims=True))
        return m_new

    def local_matmuls(ch):
        ch["h_loc"] = _dot(ch["a_loc"].astype(BF16), ch["v"])
        ch["kv"] = _dot_tn(ch["kw"].astype(BF16), ch["v"])

    def state_step(ch, n_st, h_ref):
        d = ch["d"]
        c_st = c_ref[d]
        q = ch["q"]
        qc = _dot(q, c_st.astype(BF16))
        qn = jnp.sum(q.astype(F32) * n_st, axis=1, keepdims=True)
        inter = ch["b_col"] + ch["m_in"]
        m_t = jnp.maximum(ch["m_loc"], inter)
        w_inter = jnp.exp(inter - m_t)
        w_loc = jnp.exp(ch["m_loc"] - m_t)
        den = w_inter * qn + w_loc * ch["rowsum"]
        inv = 1.0 / jnp.maximum(jnp.abs(den), jnp.exp(-m_t))
        h_ref[pl.ds(ch["t0"], L), :] = (w_inter * inv) * qc + (w_loc * inv) * ch["h_loc"]
        c_ref[d] = ch["decay"] * c_st + ch["kv"]
        return ch["decay"] * n_st + ch["ksum"]

    cpi = ML_CHUNKS_PER_STEP

    def rec_step(i, carry):
        m_f, n_f, m_b, n_b = carry
        chains = []
        for u in range(cpi):
            chains += [local_scores(0, cpi * i + u), local_scores(1, nc - 1 - cpi * i - u)]
        for ch in chains:
            local_gates(ch)
        for u in range(cpi):
            m_f = state_weights(chains[2 * u], m_f)
            m_b = state_weights(chains[2 * u + 1], m_b)
        for ch in chains:
            local_matmuls(ch)
        for u in range(cpi):
            n_f = state_step(chains[2 * u], n_f, hf_ref)
            n_b = state_step(chains[2 * u + 1], n_b, hb_ref)
        return m_f, n_f, m_b, n_b

    assert nc % cpi == 0
    m0 = jnp.full((1, 1), NEG_BIG, F32)
    n0 = jnp.zeros((1, ML_HEAD_DIM), F32)
    lax.fori_loop(0, nc // cpi, rec_step, (m0, n0, m0, n0), unroll=4)

    def out_step(c, carry):
        t0 = pl.multiple_of(c * L, L)
        hs = hf_ref[pl.ds(t0, L), :] + hb_ref[pl.ds(t0, L), :]
        mu = jnp.mean(hs, axis=-1, keepdims=True)
        hc = hs - mu
        var = jnp.mean(hc * hc, axis=-1, keepdims=True)
        hn = hc * lax.rsqrt(var + LN_EPS) * nw_ref[...]
        og = jax.nn.sigmoid(o_ref[pl.ds(t0, L), :].astype(F32))
        y_ref[pl.ds(t0, L), :] = (hn * og).astype(y_ref.dtype)
        return carry

    lax.fori_loop(0, nc, out_step, 0, unroll=4)


def _mlstm_branch(hp3, gates3, igate_b, fgate_b, norm_w):
    bsz, s, _ = hp3.shape
    assert s % ML_CHUNK == 0
    hd = ML_HEAD_DIM
    ng = 4 * ML_HEADS
    gates_t = gates3[..., :ng].transpose(0, 2, 1)
    gb = jnp.concatenate([igate_b, fgate_b]).astype(F32)
    gbl = jnp.pad(gb, (0, LANES - ng)).reshape(1, LANES)
    gbs = gb.reshape(ng, 1)
    nw = norm_w.astype(F32).reshape(1, ML_WIDTH)
    qb, kb, vb, ob = COL_ML_Q // hd, COL_ML_K // hd, COL_ML_V // hd, COL_ML_O // hd
    return pl.pallas_call(
        functools.partial(_mlstm_kernel, seq=s),
        grid=(bsz, ML_HEADS),
        in_specs=[
            pl.BlockSpec((None, s, hd), lambda b, h: (b, 0, qb + h)),
            pl.BlockSpec((None, s, hd), lambda b, h: (b, 0, kb + h)),
            pl.BlockSpec((None, s, hd), lambda b, h: (b, 0, vb + h)),
            pl.BlockSpec((None, s, hd), lambda b, h: (b, 0, ob + h)),
            pl.BlockSpec((None, s, LANES), lambda b, h: (b, 0, 0)),
            pl.BlockSpec((None, ng, s), lambda b, h: (b, 0, 0)),
            pl.BlockSpec((1, LANES), lambda b, h: (0, 0)),
            pl.BlockSpec((ng, 1), lambda b, h: (0, 0)),
            pl.BlockSpec((1, hd), lambda b, h: (0, h)),
        ],
        out_specs=pl.BlockSpec((None, s, hd), lambda b, h: (b, 0, h)),
        out_shape=jax.ShapeDtypeStruct((bsz, s, ML_WIDTH), BF16),
        scratch_shapes=[
            pltpu.VMEM((s, hd), F32),
            pltpu.VMEM((s, hd), F32),
            pltpu.VMEM((2, hd, hd), F32),
        ],
        compiler_params=pltpu.CompilerParams(
            dimension_semantics=("parallel", "parallel"), vmem_limit_bytes=VMEM_LIMIT),
        name="mlstm",
    )(hp3, hp3, hp3, hp3, gates3, gates_t, gbl, gbs, nw)


MIX_SUB_ROWS = 256


def _mix_kernel(x_ref, yna_ref, yml_ref, gna_ref, gml_ref, wna_ref, wml_ref, wo_ref, g_ref, b_ref,
                o_ref, *, alpha):
    tm = x_ref.shape[0]
    for r0 in range(0, tm, MIX_SUB_ROWS):
        r = slice(r0, r0 + MIX_SUB_ROWS)
        a = _dot(yna_ref[r, :], wna_ref[...])
        mixed = jax.nn.sigmoid(gna_ref[r, :].astype(F32)) * a
        a = _dot(yml_ref[r, :], wml_ref[...])
        mixed = mixed + jax.nn.sigmoid(gml_ref[r, :].astype(F32)) * a
        z = alpha * x_ref[r, :] + _dot(mixed.astype(BF16), wo_ref[...])
        o_ref[r, :] = _layer_norm(z, g_ref[...], b_ref[...])


def _mix(x2, yna2, yml2, hp, wna, wml, wo, g, b, alpha):
    m, d = x2.shape
    tm = min(MIX_TM, m)
    assert tm % MIX_SUB_ROWS == 0
    const = dict(pipeline_mode=pl.Buffered(1))
    return pl.pallas_call(
        functools.partial(_mix_kernel, alpha=alpha),
        grid=(m // tm,),
        in_specs=[
            pl.BlockSpec((tm, d), lambda i: (i, 0)),
            pl.BlockSpec((tm, NA_WIDTH), lambda i: (i, 0)),
            pl.BlockSpec((tm, ML_WIDTH), lambda i: (i, 0)),
            pl.BlockSpec((tm, d), lambda i: (i, COL_MERGE_NA // d)),
            pl.BlockSpec((tm, d), lambda i: (i, COL_MERGE_ML // d)),
            pl.BlockSpec((NA_WIDTH, d), lambda i: (0, 0), **const),
            pl.BlockSpec((ML_WIDTH, d), lambda i: (0, 0), **const),
            pl.BlockSpec((d, d), lambda i: (0, 0), **const),
            pl.BlockSpec((1, d), lambda i: (0, 0)),
            pl.BlockSpec((1, d), lambda i: (0, 0)),
        ],
        out_specs=pl.BlockSpec((tm, d), lambda i: (i, 0)),
        out_shape=jax.ShapeDtypeStruct((m, d), F32),
        compiler_params=pltpu.CompilerParams(
            dimension_semantics=("parallel",), vmem_limit_bytes=VMEM_LIMIT),
        name="mix",
    )(x2, yna2, yml2, hp, hp, wna, wml, wo, g, b)


FFN_TF = 512
FFN_LN_ROWS = 256


def _gelu_exact(x):
    return 0.5 * x * (1.0 + lax.erf(x * np.float32(math.sqrt(0.5))))


def _ffn_kernel(x_ref, xp_ref, xn_ref, wg_ref, wv_ref, wd_ref, cw_ref, cb_ref, g_ref, b_ref,
                o_ref, xb_ref, *, alpha, seq, tm, tail):
    i = pl.program_id(0)
    j = pl.program_id(1)
    last = pl.num_programs(1) - 1
    hl = CONV_HALO

    def glu_block(width):
        cols = slice(0, width)
        ug = _dot(xb_ref[...], wg_ref[:, cols])
        uv = _dot(xb_ref[hl:hl + tm, :], wv_ref[:, cols])
        c = _conv3_rows(ug, None, cw_ref, cb_ref, cols, tm)
        return (_gelu_exact(c) * uv).astype(BF16), cols

    @pl.when(j == 0)
    def _():
        _fill_halo_tile(xb_ref, x_ref, xp_ref, xn_ref, i, tm, seq)
        h, cols = glu_block(wg_ref.shape[1])
        o_ref[...] = _dot(h, wd_ref[cols, :])

    @pl.when((j > 0) & (j < last))
    def _():
        h, cols = glu_block(wg_ref.shape[1])
        o_ref[...] += _dot(h, wd_ref[cols, :])

    @pl.when(j == last)
    def _():
        h, cols = glu_block(tail)
        for r0 in range(0, tm, FFN_LN_ROWS):
            r = slice(r0, r0 + FFN_LN_ROWS)
            z = alpha * x_ref[r, :] + (o_ref[r, :] + _dot(h[r], wd_ref[cols, :]))
            o_ref[r, :] = _layer_norm(z, g_ref[...], b_ref[...])


def _ffn(x1, wg, wv, wd, cw, cb, g, b, alpha, seq, dff):
    m, d = x1.shape
    tm = min(FFN_TM, seq)
    tf = FFN_TF
    hl = CONV_HALO
    nff = pl.cdiv(dff, tf)
    assert seq % tm == 0 and m % tm == 0 and dff % LANES == 0 and wg.shape == (d, nff * tf) and nff >= 2
    nb = tm // hl
    last = m // hl - 1
    return pl.pallas_call(
        functools.partial(_ffn_kernel, alpha=alpha, seq=seq, tm=tm, tail=dff - (nff - 1) * tf),
        grid=(m // tm, nff),
        in_specs=[
            pl.BlockSpec((tm, d), lambda i, j: (i, 0)),
            pl.BlockSpec((hl, d), lambda i, j: (jnp.maximum(i * nb - 1, 0), 0)),
            pl.BlockSpec((hl, d), lambda i, j: (jnp.minimum((i + 1) * nb, last), 0)),
            pl.BlockSpec((d, tf), lambda i, j: (0, j)),
            pl.BlockSpec((d, tf), lambda i, j: (0, j)),
            pl.BlockSpec((tf, d), lambda i, j: (j, 0)),
            pl.BlockSpec((3, tf), lambda i, j: (0, j)),
            pl.BlockSpec((1, tf), lambda i, j: (0, j)),
            pl.BlockSpec((1, d), lambda i, j: (0, 0)),
            pl.BlockSpec((1, d), lambda i, j: (0, 0)),
        ],
        out_specs=pl.BlockSpec((tm, d), lambda i, j: (i, 0)),
        out_shape=jax.ShapeDtypeStruct((m, d), F32),
        scratch_shapes=[pltpu.VMEM((tm + 2 * hl, d), BF16)],
        compiler_params=pltpu.CompilerParams(
            dimension_semantics=("parallel", "arbitrary"), vmem_limit_bytes=VMEM_LIMIT_HIGH),
        name="ffn",
    )(x1, x1, x1, wg, wv, wd, cw, cb, g, b)


def _pad_cols(a, n):
    return jnp.pad(a, ((0, 0), (0, n - a.shape[1])))


def kernel(x, w_in, na_rpb, ml_conv_w, ml_conv_b, ml_igate_b, ml_fgate_b, ml_norm_w, w_branch_na,
           w_branch_ml, w_out, ln1_g, ln1_b, ffn_w_up, ffn_conv_w, ffn_conv_b, ffn_w_down, ln2_g, ln2_b):
    bsz, s, d = x.shape
    depth = w_in.shape[0]
    alpha = float((2 * depth) ** 0.25)
    dff = ffn_conv_w.shape[-1]
    m = bsz * s
    na3 = 3 * NA_WIDTH
    gates_at = na3 + 4 * ML_WIDTH
    merge_at = gates_at + 4 * ML_HEADS
    x2 = x.reshape(m, d)
    for l in range(depth):
        w = w_in[l]
        w_gate = _pad_cols(w[:, gates_at:merge_at], LANES).astype(BF16)
        wb = w.astype(BF16)
        hp, gates = _in_proj(x2, wb[:, merge_at:], wb, w_gate,
                             ml_conv_w[l].astype(F32), ml_conv_b[l].astype(F32).reshape(1, 2 * ML_WIDTH), s)
        hp3 = hp.reshape(bsz, s, HP_WIDTH)
        y_na = _na_attention(hp3, na_rpb[l])
        y_ml = _mlstm_branch(hp3, gates.reshape(bsz, s, LANES), ml_igate_b[l], ml_fgate_b[l],
                             ml_norm_w[l])
        x1 = _mix(x2, y_na.reshape(m, NA_WIDTH), y_ml.reshape(m, ML_WIDTH), hp,
                  w_branch_na[l].astype(BF16), w_branch_ml[l].astype(BF16), w_out[l].astype(BF16),
                  ln1_g[l].reshape(1, d), ln1_b[l].reshape(1, d), alpha)
        dffp = pl.cdiv(dff, FFN_TF) * FFN_TF
        wg = _pad_cols(ffn_w_up[l][:, :dff], dffp).astype(BF16)
        wv = _pad_cols(ffn_w_up[l][:, dff:], dffp).astype(BF16)
        wd = ffn_w_down[l].astype(BF16)
        cw = _pad_cols(ffn_conv_w[l].astype(F32), dffp)
        cb = _pad_cols(ffn_conv_b[l].astype(F32).reshape(1, dff), dffp)
        x2 = _ffn(x1, wg, wv, wd, cw, cb, ln2_g[l].reshape(1, d), ln2_b[l].reshape(1, d), alpha, s, dff)
    return x2.reshape(bsz, s, d)
```

```python
import functools
import math

import numpy as np
import jax
import jax.numpy as jnp
from jax import lax
from jax.experimental import pallas as pl
from jax.experimental.pallas import tpu as pltpu

GRID_W = 64
NA_HEADS = 16
NA_HEAD_DIM = 64
NA_WIDTH = NA_HEADS * NA_HEAD_DIM
NA_WIN_ROWS = 8
NA_WIN_COLS = 16
NA_ROW_GROUP = 2
LOG2E = math.log2(math.e)
NA_Q_SCALE = NA_HEAD_DIM ** -0.5 * LOG2E
ML_HEADS = 4
ML_HEAD_DIM = 256
ML_WIDTH = ML_HEADS * ML_HEAD_DIM
ML_CHUNK = 128
ML_CHUNKS_PER_STEP = 2
LN_EPS = 1e-5
NEG_BIG = -1e30

LANES = 128
BF16_ROWS = 16
VMEM_LIMIT = 56 * 1024 * 1024
VMEM_LIMIT_HIGH = 60 * 1024 * 1024

IN_TM = 1024
MIX_TM = 512
FFN_TM = 1024

F32 = jnp.float32
BF16 = jnp.bfloat16

COL_MERGE_NA = 0
COL_MERGE_ML = 2048
COL_NA_Q = 4096
COL_NA_K = 5120
COL_NA_V = 6144
COL_ML_V = 7168
COL_ML_O = 8192
COL_ML_Q = 9216
COL_ML_K = 10240
HP_WIDTH = 11264


def _dot(a, b):
    return jnp.dot(a, b, preferred_element_type=F32)


def _dot_nt(a, b):
    return lax.dot_general(a, b, (((1,), (1,)), ((), ())), preferred_element_type=F32)


def _dot_tn(a, b):
    return lax.dot_general(a, b, (((0,), (0,)), ((), ())), preferred_element_type=F32)


def _layer_norm(z, g, b):
    mu = jnp.mean(z, axis=-1, keepdims=True)
    zc = z - mu
    var = jnp.mean(zc * zc, axis=-1, keepdims=True)
    return zc * lax.rsqrt(var + LN_EPS) * g + b


IN_TN = 1024
CONV_HALO = BF16_ROWS
CONV_COLS = 256


def _seq_edge_flags(i, tm, seq):
    has_prev = jnp.where((i * tm) % seq != 0, 1.0, 0.0).astype(F32)
    has_next = jnp.where(((i + 1) * tm) % seq != 0, 1.0, 0.0).astype(F32)
    return has_prev, has_next


def _fill_halo_tile(xb_ref, x_ref, xp_ref, xn_ref, i, tm, seq):
    hl = CONV_HALO
    has_prev, has_next = _seq_edge_flags(i, tm, seq)
    xb_ref[0:hl, :] = (xp_ref[...] * has_prev).astype(BF16)
    xb_ref[hl:hl + tm, :] = x_ref[...].astype(BF16)
    xb_ref[hl + tm:, :] = (xn_ref[...] * has_next).astype(BF16)


def _conv3_rows(u, u_ref, cw_ref, cb_ref, cols, tm):
    hl = CONV_HALO
    if u_ref is None:
        um1 = pltpu.roll(u, 1, 0)[hl:hl + tm]
        u0 = u[hl:hl + tm]
        up1 = pltpu.roll(u, u.shape[0] - 1, 0)[hl:hl + tm]
    else:
        u_ref[...] = u
        um1 = u_ref[hl - 1:hl - 1 + tm, :]
        u0 = u_ref[hl:hl + tm, :]
        up1 = u_ref[hl + 1:hl + 1 + tm, :]
    y = um1 * cw_ref[0:1, cols]
    y = y + u0 * cw_ref[1:2, cols]
    y = y + up1 * cw_ref[2:3, cols]
    return y + cb_ref[:, cols]


def _in_proj_kernel(x_ref, xp_ref, xn_ref, wm_ref, w_ref, wg_ref, cw_ref, cb_ref, o_ref, g_ref, xb_ref,
                    ub_ref, *, seq, tm, merge_blocks, conv_blocks):
    i = pl.program_id(0)
    j = pl.program_id(1)
    hl = CONV_HALO
    q_block, k_block = conv_blocks

    is_merge = j < merge_blocks
    is_conv = (j == q_block) | (j == k_block)

    @pl.when(j == 0)
    def _():
        _fill_halo_tile(xb_ref, x_ref, xp_ref, xn_ref, i, tm, seq)
        g_ref[...] = _dot(xb_ref[hl:hl + tm, :], wg_ref[...])
        o_ref[...] = _dot(xb_ref[hl:hl + tm, :], wm_ref[...]).astype(o_ref.dtype)

    @pl.when((j > 0) & is_merge)
    def _():
        o_ref[...] = _dot(xb_ref[hl:hl + tm, :], wm_ref[...]).astype(o_ref.dtype)

    @pl.when(jnp.logical_not(is_merge | is_conv))
    def _():
        scale = jnp.where(j == COL_NA_Q // IN_TN, NA_Q_SCALE, 1.0).astype(F32)
        o_ref[...] = (_dot(xb_ref[hl:hl + tm, :], w_ref[...]) * scale).astype(o_ref.dtype)

    @pl.when(is_conv)
    def _():
        post = jnp.where(j == k_block, ML_HEAD_DIM ** -0.5, 1.0).astype(F32)
        for n, c0 in enumerate(range(0, IN_TN, CONV_COLS)):
            cols = slice(c0, c0 + CONV_COLS)
            u = _dot(xb_ref[...], w_ref[:, cols])
            y = _conv3_rows(u, ub_ref.at[n % 2], cw_ref, cb_ref, cols, tm)
            y = y * jax.nn.sigmoid(y)
            o_ref[:, cols] = (y * post).astype(o_ref.dtype)


def _in_proj(x2, w_merge, w_rest, w_gate, conv_w, conv_b, seq):
    m, d = x2.shape
    n = HP_WIDTH
    tm = min(IN_TM, seq)
    tn = IN_TN
    hl = CONV_HALO
    assert seq % tm == 0 and m % tm == 0 and COL_ML_Q % tn == 0 and COL_ML_K == COL_ML_Q + tn
    assert w_merge.shape[1] == COL_NA_Q and COL_NA_Q % tn == 0 and w_rest.shape[1] >= n - COL_NA_Q
    q_block = COL_ML_Q // tn
    merge_blocks = COL_NA_Q // tn
    rest_blocks = (n - COL_NA_Q) // tn
    nb = tm // hl
    last = m // hl - 1
    conv_idx = lambda i, j: (0, jnp.clip(j - q_block, 0, 1))
    na_blocks = 3 * NA_WIDTH // tn
    ml_blocks = ML_WIDTH // tn
    assert (COL_ML_V, COL_ML_Q) == (COL_NA_Q + 3 * NA_WIDTH, COL_NA_Q + 3 * NA_WIDTH + 2 * ML_WIDTH)

    def rest_src(r):
        return jnp.where(r < na_blocks, r,
                         jnp.where(r < na_blocks + 2 * ml_blocks, r + 2 * ml_blocks, r - 2 * ml_blocks))
    return pl.pallas_call(
        functools.partial(_in_proj_kernel, seq=seq, tm=tm, merge_blocks=merge_blocks,
                          conv_blocks=(q_block, q_block + 1)),
        grid=(m // tm, n // tn),
        in_specs=[
            pl.BlockSpec((tm, d), lambda i, j: (i, 0)),
            pl.BlockSpec((hl, d), lambda i, j: (jnp.maximum(i * nb - 1, 0), 0)),
            pl.BlockSpec((hl, d), lambda i, j: (jnp.minimum((i + 1) * nb, last), 0)),
            pl.BlockSpec((d, tn), lambda i, j: (0, jnp.minimum(j, merge_blocks - 1))),
            pl.BlockSpec((d, tn), lambda i, j: (0, rest_src(jnp.where(j < merge_blocks, rest_blocks - 1,
                                                                      j - merge_blocks)))),
            pl.BlockSpec((d, LANES), lambda i, j: (0, 0)),
            pl.BlockSpec((3, tn), conv_idx),
            pl.BlockSpec((1, tn), conv_idx),
        ],
        out_specs=[
            pl.BlockSpec((tm, tn), lambda i, j: (i, j)),
            pl.BlockSpec((tm, LANES), lambda i, j: (i, 0)),
        ],
        out_shape=[
            jax.ShapeDtypeStruct((m, n), BF16),
            jax.ShapeDtypeStruct((m, LANES), F32),
        ],
        scratch_shapes=[pltpu.VMEM((tm + 2 * hl, d), BF16),
                        pltpu.VMEM((2, tm + 2 * hl, CONV_COLS), F32)],
        compiler_params=pltpu.CompilerParams(
            dimension_semantics=("parallel", "arbitrary"), vmem_limit_bytes=VMEM_LIMIT),
        name="in_proj",
    )(x2, x2, x2, w_merge, w_rest, w_gate, conv_w, conv_b)


def _na_kernel(q_ref, k_ref, v_ref, bias_ref, o_ref, sca_ref, scb_ref, *, rows, group):
    kh = NA_WIN_ROWS
    nkeys = kh * GRID_W
    lane = lax.broadcasted_iota(jnp.int32, (2 * GRID_W, LANES), 1)
    qcol = lax.broadcasted_iota(jnp.int32, (2 * GRID_W, LANES), 0) & (GRID_W - 1)
    kcol = lane & (GRID_W - 1)
    wstart = jnp.clip(qcol - NA_WIN_COLS // 2, 0, GRID_W - NA_WIN_COLS)
    valid = (kcol >= wstart) & (kcol < wstart + NA_WIN_COLS)
    head_a = lax.broadcasted_iota(jnp.int32, (GRID_W, LANES), 1) < NA_HEAD_DIM

    def window(r):
        rs = min(max(r - kh // 2, 0), rows - kh)
        return rs * GRID_W, rs - r + (NA_WIN_ROWS - 1)

    def score_stage(g, sc_ref):
        for u in range(group):
            r = g * group + u
            kbase, _ = window(r)
            q = q_ref[pl.ds(r * GRID_W, GRID_W), :]
            zero = jnp.zeros_like(q)
            qs = jnp.concatenate([jnp.where(head_a, q, zero), jnp.where(head_a, zero, q)], axis=0)
            sc_ref[u] = _dot_nt(qs, k_ref[pl.ds(kbase, nkeys), :])

    def value_stage(g, sc_ref):
        for u in range(group):
            r = g * group + u
            kbase, dr0 = window(r)
            s = []
            for c in range(kh // 2):
                bias = jnp.concatenate([bias_ref[0, dr0 + 2 * c], bias_ref[1, dr0 + 2 * c]], axis=0)
                sc_c = sc_ref[u, :, c * LANES:(c + 1) * LANES] + bias
                s.append(jnp.where(valid, sc_c, NEG_BIG))
            m = jnp.maximum(jnp.maximum(s[0], s[1]), jnp.maximum(s[2], s[3]))
            m = jnp.max(m, axis=-1, keepdims=True)
            p = [jnp.exp2(s_c - m) for s_c in s]
            l = jnp.sum((p[0] + p[1]) + (p[2] + p[3]), axis=-1, keepdims=True)
            pb = jnp.concatenate([p_c.astype(BF16) for p_c in p], axis=1)
            acc = _dot(pb, v_ref[pl.ds(kbase, nkeys), :]) / l
            o = jnp.where(head_a, acc[:GRID_W], acc[GRID_W:])
            o_ref[pl.ds(r * GRID_W, GRID_W), :] = o.astype(o_ref.dtype)

    ngroups = rows // group
    bufs = (sca_ref, scb_ref)
    score_stage(0, bufs[0])
    for g in range(ngroups):
        if g + 1 < ngroups:
            score_stage(g + 1, bufs[(g + 1) % 2])
        value_stage(g, bufs[g % 2])


def _na_bias_expansion():
    nrel = 2 * NA_WIN_COLS - 1
    qc = np.arange(GRID_W)[:, None]
    kc = np.arange(GRID_W)[None, :]
    dc = np.clip(kc - qc + NA_WIN_COLS - 1, 0, nrel - 1)
    oh = np.zeros((2, nrel, GRID_W, 2, GRID_W), np.float32)
    for half in range(2):
        oh[half, dc, qc, half, kc] = 1.0
    return oh.reshape(2 * nrel, GRID_W * LANES)


def _na_bias_table(rpb):
    r2 = jnp.concatenate([rpb[:, :-1], rpb[:, 1:]], axis=-1)
    r2 = r2.reshape(NA_HEADS * (2 * NA_WIN_ROWS - 2), -1)
    t = jnp.dot(r2, jnp.asarray(_na_bias_expansion()), precision=lax.Precision.HIGHEST)
    return t.reshape(NA_HEADS // 2, 2, 2 * NA_WIN_ROWS - 2, GRID_W, LANES)


def _na_attention(hp3, rpb):
    bsz, s, _ = hp3.shape
    rows = s // GRID_W
    assert s % GRID_W == 0 and rows >= NA_WIN_ROWS
    table = _na_bias_table(rpb.astype(F32) * LOG2E)
    qb, kb, vb = COL_NA_Q // LANES, COL_NA_K // LANES, COL_NA_V // LANES
    group = min(NA_ROW_GROUP, rows // 2)
    assert rows % (2 * group) == 0
    sc_buf = pltpu.VMEM((group, 2 * GRID_W, NA_WIN_ROWS * GRID_W), F32)
    return pl.pallas_call(
        functools.partial(_na_kernel, rows=rows, group=group),
        grid=(bsz, NA_HEADS // 2),
        in_specs=[
            pl.BlockSpec((None, s, LANES), lambda b, h: (b, 0, qb + h)),
            pl.BlockSpec((None, s, LANES), lambda b, h: (b, 0, kb + h)),
            pl.BlockSpec((None, s, LANES), lambda b, h: (b, 0, vb + h)),
            pl.BlockSpec((None, 2, 2 * NA_WIN_ROWS - 2, GRID_W, LANES), lambda b, h: (h, 0, 0, 0, 0)),
        ],
        out_specs=pl.BlockSpec((None, s, LANES), lambda b, h: (b, 0, h)),
        out_shape=jax.ShapeDtypeStruct((bsz, s, NA_WIDTH), BF16),
        scratch_shapes=[sc_buf, sc_buf],
        compiler_params=pltpu.CompilerParams(
            dimension_semantics=("parallel", "parallel"), vmem_limit_bytes=VMEM_LIMIT),
        name="na_attn",
    )(hp3, hp3, hp3, table)


def _log_sigmoid(x):
    return jnp.minimum(x, 0.0) - jnp.log1p(jnp.exp(-jnp.abs(x)))


def _mlstm_kernel(q_ref, k_ref, v_ref, o_ref, g_ref, gt_ref, gbl_ref, gbs_ref, nw_ref, y_ref,
                  hf_ref, hb_ref, c_ref, *, seq):
    L = ML_CHUNK
    nc = seq // L
    head = pl.program_id(1)
    row = lax.broadcasted_iota(jnp.int32, (L, L), 0)
    col = lax.broadcasted_iota(jnp.int32, (L, L), 1)
    tril = row >= col
    triu = row <= col
    ng = 2 * ML_HEADS
    sub8 = lax.broadcasted_iota(jnp.int32, (ng, L), 0)

    c_ref[...] = jnp.zeros_like(c_ref)

    eye = row == col

    def local_scores(d, c):
        t0 = pl.multiple_of(c * L, L)
        q = q_ref[pl.ds(t0, L), :]
        k = k_ref[pl.ds(t0, L), :]
        v = v_ref[pl.ds(t0, L), :]
        return dict(d=d, t0=t0, q=q, k=k, v=v, s_qk=_dot_nt(q, k))

    def local_gates(ch):
        d, t0 = ch["d"], ch["t0"]
        valid = tril if d == 0 else triu
        ich = d * ML_HEADS + head
        pick = sub8 == ich
        gi = gt_ref[0:ng, pl.ds(t0, L)] + gbs_ref[0:ng, :]
        gf = _log_sigmoid(gt_ref[ng:2 * ng, pl.ds(t0, L)] + gbs_ref[ng:2 * ng, :])
        li_row = jnp.sum(jnp.where(pick, gi, 0.0), axis=0, keepdims=True)
        lf_row = jnp.sum(jnp.where(pick, gf, 0.0), axis=0, keepdims=True)
        g_tile = g_ref[pl.ds(t0, L), :] + gbl_ref[...]
        li_col = jnp.sum(jnp.where(col == ich, g_tile, 0.0), axis=1, keepdims=True)
        b_col = jnp.sum(jnp.where(valid, lf_row, 0.0), axis=1, keepdims=True)
        b_row = jnp.sum(jnp.where(eye, b_col, 0.0), axis=0, keepdims=True)
        b_last = jnp.sum(lf_row, axis=1, keepdims=True)
        dlog = jnp.where(valid, b_col - b_row + li_row, NEG_BIG)
        m_loc = jnp.max(dlog, axis=1, keepdims=True)
        a_loc = ch["s_qk"] * jnp.exp(dlog - m_loc)
        logw = b_last - b_col + li_col
        ch.update(b_col=b_col, b_last=b_last, m_loc=m_loc, a_loc=a_loc,
                  rowsum=jnp.sum(a_loc, axis=1, keepdims=True), logw=logw,
                  m_w=jnp.max(logw, axis=0, keepdims=True))

    def state_weights(ch, m_st):
        m_new = jnp.maximum(ch["b_last"] + m_st, ch["m_w"])
        kw = ch["k"].astype(F32) * jnp.exp(ch["logw"] - m_new)
        ch.update(m_in=m_st, decay=jnp.exp(ch["b_last"] + m_st - m_new), kw=kw,
                  ksum=jnp.sum(kw, axis=0, keepdims=True))
        return m_new

    def local_matmuls(ch):
        ch["h_loc"] = _dot(ch["a_loc"].astype(BF16), ch["v"])
        ch["kv"] = _dot_tn(ch["kw"].astype(BF16), ch["v"])

    def state_step(ch, n_st, h_ref):
        d = ch["d"]
        c_st = c_ref[d]
        q = ch["q"]
        qc = _dot(q, c_st.astype(BF16))
        qn = jnp.sum(q.astype(F32) * n_st, axis=1, keepdims=True)
        inter = ch["b_col"] + ch["m_in"]
        m_t = jnp.maximum(ch["m_loc"], inter)
        w_inter = jnp.exp(inter - m_t)
        w_loc = jnp.exp(ch["m_loc"] - m_t)
        den = w_inter * qn + w_loc * ch["rowsum"]
        inv = 1.0 / jnp.maximum(jnp.abs(den), jnp.exp(-m_t))
        h_ref[pl.ds(ch["t0"], L), :] = (w_inter * inv) * qc + (w_loc * inv) * ch["h_loc"]
        c_ref[d] = ch["decay"] * c_st + ch["kv"]
        return ch["decay"] * n_st + ch["ksum"]

    cpi = ML_CHUNKS_PER_STEP

    def rec_step(i, carry):
        m_f, n_f, m_b, n_b = carry
        chains = []
        for u in range(cpi):
            chains += [local_scores(0, cpi * i + u), local_scores(1, nc - 1 - cpi * i - u)]
        for ch in chains:
            local_gates(ch)
        for u in range(cpi):
            m_f = state_weights(chains[2 * u], m_f)
            m_b = state_weights(chains[2 * u + 1], m_b)
        for ch in chains:
            local_matmuls(ch)
        for u in range(cpi):
            n_f = state_step(chains[2 * u], n_f, hf_ref)
            n_b = state_step(chains[2 * u + 1], n_b, hb_ref)
        return m_f, n_f, m_b, n_b

    assert nc % cpi == 0
    m0 = jnp.full((1, 1), NEG_BIG, F32)
    n0 = jnp.zeros((1, ML_HEAD_DIM), F32)
    lax.fori_loop(0, nc // cpi, rec_step, (m0, n0, m0, n0), unroll=4)

    def out_step(c, carry):
        t0 = pl.multiple_of(c * L, L)
        hs = hf_ref[pl.ds(t0, L), :] + hb_ref[pl.ds(t0, L), :]
        mu = jnp.mean(hs, axis=-1, keepdims=True)
        hc = hs - mu
        var = jnp.mean(hc * hc, axis=-1, keepdims=True)
        hn = hc * lax.rsqrt(var + LN_EPS) * nw_ref[...]
        og = jax.nn.sigmoid(o_ref[pl.ds(t0, L), :].astype(F32))
        y_ref[pl.ds(t0, L), :] = (hn * og).astype(y_ref.dtype)
        return carry

    lax.fori_loop(0, nc, out_step, 0, unroll=4)


def _mlstm_branch(hp3, gates3, igate_b, fgate_b, norm_w):
    bsz, s, _ = hp3.shape
    assert s % ML_CHUNK == 0
    hd = ML_HEAD_DIM
    ng = 4 * ML_HEADS
    gates_t = gates3[..., :ng].transpose(0, 2, 1)
    gb = jnp.concatenate([igate_b, fgate_b]).astype(F32)
    gbl = jnp.pad(gb, (0, LANES - ng)).reshape(1, LANES)
    gbs = gb.reshape(ng, 1)
    nw = norm_w.astype(F32).reshape(1, ML_WIDTH)
    qb, kb, vb, ob = COL_ML_Q // hd, COL_ML_K // hd, COL_ML_V // hd, COL_ML_O // hd
    return pl.pallas_call(
        functools.partial(_mlstm_kernel, seq=s),
        grid=(bsz, ML_HEADS),
        in_specs=[
            pl.BlockSpec((None, s, hd), lambda b, h: (b, 0, qb + h)),
            pl.BlockSpec((None, s, hd), lambda b, h: (b, 0, kb + h)),
            pl.BlockSpec((None, s, hd), lambda b, h: (b, 0, vb + h)),
            pl.BlockSpec((None, s, hd), lambda b, h: (b, 0, ob + h)),
            pl.BlockSpec((None, s, LANES), lambda b, h: (b, 0, 0)),
            pl.BlockSpec((None, ng, s), lambda b, h: (b, 0, 0)),
            pl.BlockSpec((1, LANES), lambda b, h: (0, 0)),
            pl.BlockSpec((ng, 1), lambda b, h: (0, 0)),
            pl.BlockSpec((1, hd), lambda b, h: (0, h)),
        ],
        out_specs=pl.BlockSpec((None, s, hd), lambda b, h: (b, 0, h)),
        out_shape=jax.ShapeDtypeStruct((bsz, s, ML_WIDTH), BF16),
        scratch_shapes=[
            pltpu.VMEM((s, hd), F32),
            pltpu.VMEM((s, hd), F32),
            pltpu.VMEM((2, hd, hd), F32),
        ],
        compiler_params=pltpu.CompilerParams(
            dimension_semantics=("parallel", "parallel"), vmem_limit_bytes=VMEM_LIMIT),
        name="mlstm",
    )(hp3, hp3, hp3, hp3, gates3, gates_t, gbl, gbs, nw)


MIX_SUB_ROWS = 256


def _mix_kernel(x_ref, yna_ref, yml_ref, gna_ref, gml_ref, wna_ref, wml_ref, wo_ref, g_ref, b_ref,
                o_ref, z_ref, *, alpha):
    @pl.when(pl.program_id(0) == 0)
    def _():
        z_ref[...] = jnp.zeros_like(z_ref)

    o_ref[...] = _layer_norm(z_ref[...], g_ref[...], b_ref[...])
    tm = x_ref.shape[0]
    for r0 in range(0, tm, MIX_SUB_ROWS):
        r = slice(r0, r0 + MIX_SUB_ROWS)
        a = _dot(yna_ref[r, :], wna_ref[...])
        mixed = jax.nn.sigmoid(gna_ref[r, :].astype(F32)) * a
        a = _dot(yml_ref[r, :], wml_ref[...])
        mixed = mixed + jax.nn.sigmoid(gml_ref[r, :].astype(F32)) * a
        z_ref[r, :] = alpha * x_ref[r, :] + _dot(mixed.astype(BF16), wo_ref[...])


def _mix(x2, yna2, yml2, hp, wna, wml, wo, g, b, alpha):
    m, d = x2.shape
    tm = min(MIX_TM, m)
    assert tm % MIX_SUB_ROWS == 0
    const = dict(pipeline_mode=pl.Buffered(1))
    nt = m // tm
    cur = lambda i: jnp.minimum(i, nt - 1)
    return pl.pallas_call(
        functools.partial(_mix_kernel, alpha=alpha),
        grid=(nt + 1,),
        in_specs=[
            pl.BlockSpec((tm, d), lambda i: (cur(i), 0)),
            pl.BlockSpec((tm, NA_WIDTH), lambda i: (cur(i), 0)),
            pl.BlockSpec((tm, ML_WIDTH), lambda i: (cur(i), 0)),
            pl.BlockSpec((tm, d), lambda i: (cur(i), COL_MERGE_NA // d)),
            pl.BlockSpec((tm, d), lambda i: (cur(i), COL_MERGE_ML // d)),
            pl.BlockSpec((NA_WIDTH, d), lambda i: (0, 0), **const),
            pl.BlockSpec((ML_WIDTH, d), lambda i: (0, 0), **const),
            pl.BlockSpec((d, d), lambda i: (0, 0), **const),
            pl.BlockSpec((1, d), lambda i: (0, 0)),
            pl.BlockSpec((1, d), lambda i: (0, 0)),
        ],
        out_specs=pl.BlockSpec((tm, d), lambda i: (jnp.maximum(i - 1, 0), 0)),
        out_shape=jax.ShapeDtypeStruct((m, d), F32),
        scratch_shapes=[pltpu.VMEM((tm, d), F32)],
        compiler_params=pltpu.CompilerParams(
            dimension_semantics=("arbitrary",), vmem_limit_bytes=VMEM_LIMIT),
        name="mix",
    )(x2, yna2, yml2, hp, hp, wna, wml, wo, g, b)


FFN_TF = 512
FFN_LN_ROWS = 256


def _gelu_exact(x):
    return 0.5 * x * (1.0 + lax.erf(x * np.float32(math.sqrt(0.5))))


def _ffn_kernel(x_ref, xp_ref, xn_ref, wg_ref, wv_ref, wd_ref, cw_ref, cb_ref, g_ref, b_ref,
                o_ref, xb_ref, *, alpha, seq, tm, tail):
    i = pl.program_id(0)
    j = pl.program_id(1)
    last = pl.num_programs(1) - 1
    hl = CONV_HALO

    def glu_block(width):
        cols = slice(0, width)
        ug = _dot(xb_ref[...], wg_ref[:, cols])
        uv = _dot(xb_ref[hl:hl + tm, :], wv_ref[:, cols])
        c = _conv3_rows(ug, None, cw_ref, cb_ref, cols, tm)
        return (_gelu_exact(c) * uv).astype(BF16), cols

    @pl.when(j == 0)
    def _():
        _fill_halo_tile(xb_ref, x_ref, xp_ref, xn_ref, i, tm, seq)
        h, cols = glu_block(wg_ref.shape[1])
        o_ref[...] = _dot(h, wd_ref[cols, :])

    @pl.when((j > 0) & (j < last))
    def _():
        h, cols = glu_block(wg_ref.shape[1])
        o_ref[...] += _dot(h, wd_ref[cols, :])

    @pl.when(j == last)
    def _():
        h, cols = glu_block(tail)
        for r0 in range(0, tm, FFN_LN_ROWS):
            r = slice(r0, r0 + FFN_LN_ROWS)
            z = alpha * x_ref[r, :] + (o_ref[r, :] + _dot(h[r], wd_ref[cols, :]))
            o_ref[r, :] = _layer_norm(z, g_ref[...], b_ref[...])


def _ffn(x1, wg, wv, wd, cw, cb, g, b, alpha, seq, dff):
    m, d = x1.shape
    tm = min(FFN_TM, seq)
    tf = FFN_TF
    hl = CONV_HALO
    nff = pl.cdiv(dff, tf)
    assert seq % tm == 0 and m % tm == 0 and dff % LANES == 0 and wg.shape == (d, nff * tf) and nff >= 2
    nb = tm // hl
    last = m // hl - 1
    return pl.pallas_call(
        functools.partial(_ffn_kernel, alpha=alpha, seq=seq, tm=tm, tail=dff - (nff - 1) * tf),
        grid=(m // tm, nff),
        in_specs=[
            pl.BlockSpec((tm, d), lambda i, j: (i, 0)),
            pl.BlockSpec((hl, d), lambda i, j: (jnp.maximum(i * nb - 1, 0), 0)),
            pl.BlockSpec((hl, d), lambda i, j: (jnp.minimum((i + 1) * nb, last), 0)),
            pl.BlockSpec((d, tf), lambda i, j: (0, j)),
            pl.BlockSpec((d, tf), lambda i, j: (0, j)),
            pl.BlockSpec((tf, d), lambda i, j: (j, 0)),
            pl.BlockSpec((3, tf), lambda i, j: (0, j)),
            pl.BlockSpec((1, tf), lambda i, j: (0, j)),
            pl.BlockSpec((1, d), lambda i, j: (0, 0)),
            pl.BlockSpec((1, d), lambda i, j: (0, 0)),
        ],
        out_specs=pl.BlockSpec((tm, d), lambda i, j: (i, 0)),
        out_shape=jax.ShapeDtypeStruct((m, d), F32),
        scratch_shapes=[pltpu.VMEM((tm + 2 * hl, d), BF16)],
        compiler_params=pltpu.CompilerParams(
            dimension_semantics=("parallel", "arbitrary"), vmem_limit_bytes=VMEM_LIMIT_HIGH),
        name="ffn",
    )(x1, x1, x1, wg, wv, wd, cw, cb, g, b)


def _pad_cols(a, n):
    return jnp.pad(a, ((0, 0), (0, n - a.shape[1])))


def kernel(x, w_in, na_rpb, ml_conv_w, ml_conv_b, ml_igate_b, ml_fgate_b, ml_norm_w, w_branch_na,
           w_branch_ml, w_out, ln1_g, ln1_b, ffn_w_up, ffn_conv_w, ffn_conv_b, ffn_w_down, ln2_g, ln2_b):
    bsz, s, d = x.shape
    depth = w_in.shape[0]
    alpha = float((2 * depth) ** 0.25)
    dff = ffn_conv_w.shape[-1]
    m = bsz * s
    na3 = 3 * NA_WIDTH
    gates_at = na3 + 4 * ML_WIDTH
    merge_at = gates_at + 4 * ML_HEADS
    x2 = x.reshape(m, d)
    for l in range(depth):
        w = w_in[l]
        w_gate = _pad_cols(w[:, gates_at:merge_at], LANES).astype(BF16)
        wb = w.astype(BF16)
        hp, gates = _in_proj(x2, wb[:, merge_at:], wb, w_gate,
                             ml_conv_w[l].astype(F32), ml_conv_b[l].astype(F32).reshape(1, 2 * ML_WIDTH), s)
        hp3 = hp.reshape(bsz, s, HP_WIDTH)
        y_na = _na_attention(hp3, na_rpb[l])
        y_ml = _mlstm_branch(hp3, gates.reshape(bsz, s, LANES), ml_igate_b[l], ml_fgate_b[l],
                             ml_norm_w[l])
        x1 = _mix(x2, y_na.reshape(m, NA_WIDTH), y_ml.reshape(m, ML_WIDTH), hp,
                  w_branch_na[l].astype(BF16), w_branch_ml[l].astype(BF16), w_out[l].astype(BF16),
                  ln1_g[l].reshape(1, d), ln1_b[l].reshape(1, d), alpha)
        dffp = pl.cdiv(dff, FFN_TF) * FFN_TF
        wg = _pad_cols(ffn_w_up[l][:, :dff], dffp).astype(BF16)
        wv = _pad_cols(ffn_w_up[l][:, dff:], dffp).astype(BF16)
        wd = ffn_w_down[l].astype(BF16)
        cw = _pad_cols(ffn_conv_w[l].astype(F32), dffp)
        cb = _pad_cols(ffn_conv_b[l].astype(F32).reshape(1, dff), dffp)
        x2 = _ffn(x1, wg, wv, wd, cw, cb, ln2_g[l].reshape(1, d), ln2_b[l].reshape(1, d), alpha, s, dff)
    return x2.reshape(bsz, s, d)
```
